```python
import functools
import jax
import jax.numpy as jnp
from jax import lax
import numpy as np

D_MODEL = 1024
BATCH = 4
SEQ = 8192
DEPTH = 4
DEC_BATCH = 8
DEC_SEQ = 2048
PAST_LEN = 128

GRID_W = 64
N_MIXERS = 4
N_OCC = DEPTH // N_MIXERS
EPS = 1e-6

NA_HEADS = 16
NA_HEAD_DIM = D_MODEL // NA_HEADS
NA_WIN_ROWS = 8
NA_WIN_COLS = 16

HG_HEADS = 8
HG_KEY_DIM = D_MODEL // HG_HEADS
HG_VAL_DIM = D_MODEL // HG_HEADS
HG_CHUNK = 64

GQA_HEADS = 16
GQA_KV_HEADS = 4
GQA_HEAD_DIM = D_MODEL // GQA_HEADS
Q_BLOCK = 128
ROPE_THETA = 10000.0

RET_HEADS = 4
RET_KEY_DIM = D_MODEL // RET_HEADS
RET_VAL_DIM = 2 * RET_KEY_DIM
RET_CHUNK = 128

N_EXPERTS = 16
EXPERT_FF = 2 * D_MODEL
CAPACITY_FACTOR = 2

kernel_name = 'hybrid_bidir_encoder_ec_moe'


def rms_norm(x, g):
    xf = x.astype(jnp.float32)
    y = xf * lax.rsqrt(jnp.mean(xf * xf, axis=-1, keepdims=True) + EPS)
    return (y * g.astype(jnp.float32)).astype(x.dtype)


def _rotate(x, pos):
    n_freq = x.shape[-1] // 2
    inv_freq = ROPE_THETA ** (-jnp.arange(n_freq, dtype=jnp.float32) / n_freq)
    ang = pos[:, None] * inv_freq[None, :]
    cos = jnp.cos(ang)[:, None, :]
    sin = jnp.sin(ang)[:, None, :]
    x1, x2 = x[..., :n_freq], x[..., n_freq:]
    return jnp.concatenate([x1 * cos - x2 * sin, x1 * sin + x2 * cos], axis=-1)


def axial_rope(x):
    seq_len, dh = x.shape[1], x.shape[-1]
    t = jnp.arange(seq_len)
    rows = (t // GRID_W).astype(jnp.float32)
    cols = (t % GRID_W).astype(jnp.float32)
    xf = x.astype(jnp.float32)
    half = dh // 2
    out = jnp.concatenate([_rotate(xf[..., :half], rows), _rotate(xf[..., half:], cols)], axis=-1)
    return out.astype(x.dtype)


def neighbourhood_attention(u, w_in, q_gain, k_gain, rel_bias, w_out):
    bsz, seq_len, _ = u.shape
    rows = seq_len // GRID_W
    wr = min(NA_WIN_ROWS, rows)
    q, k, v = jnp.split(u @ w_in, 3, axis=-1)
    shp = (bsz, rows, GRID_W, NA_HEADS, NA_HEAD_DIM)
    q = rms_norm(q.reshape(shp), q_gain) * NA_HEAD_DIM ** -0.5
    k = rms_norm(k.reshape(shp), k_gain)
    v = v.reshape(shp)
    r_idx = jnp.arange(rows)
    c_idx = jnp.arange(GRID_W)
    r_start = jnp.clip(r_idx - wr // 2, 0, rows - wr)
    c_start = jnp.clip(c_idx - NA_WIN_COLS // 2, 0, GRID_W - NA_WIN_COLS)
    key_cols = c_start[:, None] + jnp.arange(NA_WIN_COLS)[None, :]
    col_off = key_cols - c_idx[:, None] + (NA_WIN_COLS - 1)

    def one_row(r):
        key_rows = r_start[r] + jnp.arange(wr)
        row_off = key_rows - r + (NA_WIN_ROWS - 1)
        k_g = jnp.take(jnp.take(k, key_rows, axis=1), key_cols, axis=2)
        v_g = jnp.take(jnp.take(v, key_rows, axis=1), key_cols, axis=2)
        q_r = lax.dynamic_index_in_dim(q, r, axis=1, keepdims=False)
        s = jnp.einsum('bchd,bicjhd->bhcij', q_r, k_g).astype(jnp.float32)
        bias = rel_bias[:, row_off][:, :, col_off].transpose(0, 2, 1, 3)
        s = s + bias[None].astype(jnp.float32)
        p = jax.nn.softmax(s.reshape(bsz, NA_HEADS, GRID_W, wr * NA_WIN_COLS), axis=-1)
        p = p.reshape(bsz, NA_HEADS, GRID_W, wr, NA_WIN_COLS).astype(v.dtype)
        return jnp.einsum('bhcij,bicjhd->bchd', p, v_g)

    out = lax.map(one_row, r_idx)
    out = jnp.moveaxis(out, 0, 1).reshape(bsz, seq_len, NA_HEADS * NA_HEAD_DIM)
    return out @ w_out


def _gla_direction(q, k, v, log_f):
    bsz, n_heads, seq_len, dk = q.shape
    dv = v.shape[-1]
    n_chunks = seq_len // HG_CHUNK

    def to_chunks(t):
        return t.reshape(bsz, n_heads, n_chunks, HG_CHUNK, t.shape[-1]).transpose(2, 0, 1, 3, 4)

    lower = jnp.tril(jnp.ones((HG_CHUNK, HG_CHUNK), dtype=bool))[None, None, :, :, None]

    def step(state, inp):
        qc, kc, vc, gc = inp
        b = jnp.cumsum(gc, axis=2)
        b_last = b[:, :, -1:, :]
        o_inter = jnp.einsum('bhtk,bhkv->bhtv', qc * jnp.exp(b), state)
        rel = b[:, :, :, None, :] - b[:, :, None, :, :]
        decay = jnp.exp(jnp.where(lower, rel, -jnp.inf))
        scores = jnp.einsum('bhtk,bhtsk,bhsk->bhts', qc, decay, kc)
        o = o_inter + jnp.einsum('bhts,bhsv->bhtv', scores, vc)
        state = state * jnp.exp(b_last[:, :, 0, :, None]) + jnp.einsum('bhsk,bhsv->bhkv', kc * jnp.exp(b_last - b), vc)
        return state, o

    init = jnp.zeros((bsz, n_heads, dk, dv), jnp.float32)
    _, o = lax.scan(step, init, (to_chunks(q), to_chunks(k), to_chunks(v), to_chunks(log_f)))
    return o.transpose(1, 2, 0, 3, 4).reshape(bsz, n_heads, seq_len, dv)


def hgrn2_bidirectional(u, w_in, lb_param, layer_idx, o_gain, w_out):
    bsz, seq_len, _ = u.shape
    q, i, f_fwd, f_bwd, g = jnp.split(u @ w_in, 5, axis=-1)
    lb_cum = jnp.cumsum(jax.nn.softmax(lb_param.astype(jnp.float32), axis=0), axis=0)
    lb = lb_cum[layer_idx] - lb_cum[0]

    def heads(t, d):
        return t.astype(jnp.float32).reshape(bsz, seq_len, HG_HEADS, d).transpose(0, 2, 1, 3)

    qh = heads(q, HG_KEY_DIM) * HG_KEY_DIM ** -0.5
    vh = heads(jax.nn.silu(i), HG_VAL_DIM)

    def flip(t):
        return jnp.flip(t, axis=2)

    def direction(f_logits, reverse):
        f = lb + (1.0 - lb) * jax.nn.sigmoid(f_logits.astype(jnp.float32))
        kh = heads(1.0 - f, HG_KEY_DIM)
        log_f = heads(jnp.log(f), HG_KEY_DIM)
        if reverse:
            return flip(_gla_direction(flip(qh), flip(kh), flip(vh), flip(log_f)))
        return _gla_direction(qh, kh, vh, log_f)

    o = direction(f_fwd, False) + direction(f_bwd, True)
    o = rms_norm(o.transpose(0, 2, 1, 3), o_gain).reshape(bsz, seq_len, HG_HEADS * HG_VAL_DIM)
    o = o.astype(u.dtype) * jax.nn.sigmoid(g)
    return o @ w_out


def gqa_axial(u, w_in, q_gain, k_gain, w_out):
    bsz, seq_len, _ = u.shape
    qd = GQA_HEADS * GQA_HEAD_DIM
    kd = GQA_KV_HEADS * GQA_HEAD_DIM
    group = GQA_HEADS // GQA_KV_HEADS
    n_blocks = seq_len // Q_BLOCK
    proj = u @ w_in
    q = proj[..., :qd].reshape(bsz, seq_len, GQA_HEADS, GQA_HEAD_DIM)
    k = proj[..., qd:qd + kd].reshape(bsz, seq_len, GQA_KV_HEADS, GQA_HEAD_DIM)
    v = proj[..., qd + kd:].reshape(bsz, seq_len, GQA_KV_HEADS, GQA_HEAD_DIM)
    q = axial_rope(rms_norm(q, q_gain)) * GQA_HEAD_DIM ** -0.5
    k = axial_rope(rms_norm(k, k_gain))
    qb_all = q.reshape(bsz, n_blocks, Q_BLOCK, GQA_KV_HEADS, group, GQA_HEAD_DIM).transpose(1, 0, 2, 3, 4, 5)

    def block(qb):
        s = jnp.einsum('bqkgd,bskd->bkgqs', qb, k).astype(jnp.float32)
        p = jax.nn.softmax(s, axis=-1).astype(v.dtype)
        return jnp.einsum('bkgqs,bskd->bqkgd', p, v)

    o = lax.map(block, qb_all)
    o = o.transpose(1, 0, 2, 3, 4, 5).reshape(bsz, seq_len, qd)
    return o @ w_out


def _retention_direction(q, k, v, log_gamma):
    bsz, n_heads, seq_len, dk = q.shape
    dv = v.shape[-1]
    n_chunks = seq_len // RET_CHUNK
    pos = jnp.arange(RET_CHUNK, dtype=jnp.float32)
    rel = pos[:, None] - pos[None, :]
    intra = jnp.where(rel[None] >= 0, jnp.exp(rel[None] * log_gamma[:, None, None]), 0.0)
    q_decay = jnp.exp((pos[None, :] + 1.0) * log_gamma[:, None])[:, :, None]
    k_decay = jnp.exp((RET_CHUNK - 1.0 - pos[None, :]) * log_gamma[:, None])[:, :, None]
    chunk_decay = jnp.exp(RET_CHUNK * log_gamma)[:, None, None]

    def to_chunks(t):
        return t.reshape(bsz, n_heads, n_chunks, RET_CHUNK, t.shape[-1]).transpose(2, 0, 1, 3, 4)

    def step(state, inp):
        qc, kc, vc = inp
        scores = jnp.einsum('bhtk,bhsk->bhts', qc, kc) * intra
        o = jnp.einsum('bhts,bhsv->bhtv', scores, vc) + jnp.einsum('bhtk,bhkv->bhtv', qc * q_decay, state)
        state = state * chunk_decay + jnp.einsum('bhsk,bhsv->bhkv', kc * k_decay, vc)
        return state, o

    init = jnp.zeros((bsz, n_heads, dk, dv), jnp.float32)
    _, o = lax.scan(step, init, (to_chunks(q), to_chunks(k), to_chunks(v)))
    return o.transpose(1, 2, 0, 3, 4).reshape(bsz, n_heads, seq_len, dv)


def retention_bidirectional(u, w_in, o_gain, w_out):
    bsz, seq_len, _ = u.shape
    qd = RET_HEADS * RET_KEY_DIM
    vd = RET_HEADS * RET_VAL_DIM
    proj = u @ w_in
    q = proj[..., :qd].reshape(bsz, seq_len, RET_HEADS, RET_KEY_DIM)
    k = proj[..., qd:2 * qd].reshape(bsz, seq_len, RET_HEADS, RET_KEY_DIM)
    v = proj[..., 2 * qd:2 * qd + vd].reshape(bsz, seq_len, RET_HEADS, RET_VAL_DIM)
    g = proj[..., 2 * qd + vd:]
    q = axial_rope(q) * RET_KEY_DIM ** -0.5
    k = axial_rope(k)

    def heads(t):
        return t.astype(jnp.float32).transpose(0, 2, 1, 3)

    def flip(t):
        return jnp.flip(t, axis=2)

    qh, kh, vh = heads(q), heads(k), heads(v)
    log_gamma = jnp.log1p(-jnp.exp2(-5.0 - jnp.arange(RET_HEADS, dtype=jnp.float32)))
    o = _retention_direction(qh, kh, vh, log_gamma) + flip(_retention_direction(flip(qh), flip(kh), flip(vh), log_gamma))
    o = rms_norm(o.transpose(0, 2, 1, 3), o_gain).reshape(bsz, seq_len, vd).astype(u.dtype)
    return (jax.nn.silu(g) * o) @ w_out


def expert_choice_ffn(u, w_router, w_gate, w_up, w_down):
    bsz, seq_len, d = u.shape
    n_tok = bsz * seq_len
    cap = CAPACITY_FACTOR * n_tok // N_EXPERTS
    xt = u.reshape(n_tok, d)
    aff = jax.nn.softmax((xt @ w_router).astype(jnp.float32), axis=-1)
    gate, idx = lax.top_k(aff.T, cap)
    xs = xt[idx]
    hid = jax.nn.silu(jnp.einsum('ecd,edf->ecf', xs, w_gate)) * jnp.einsum('ecd,edf->ecf', xs, w_up)
    ye = jnp.einsum('ecf,efd->ecd', hid, w_down) * gate[..., None].astype(xs.dtype)
    out = jnp.zeros_like(xt).at[idx.reshape(-1)].add(ye.reshape(-1, d))
    return out.reshape(bsz, seq_len, d)


def trunk(x, norm_mix, norm_ffn, na_w_in, na_q_gain, na_k_gain, na_rel_bias, na_w_out,
          hg_w_in, hg_lb, hg_o_gain, hg_w_out, gq_w_in, gq_q_gain, gq_k_gain, gq_w_out,
          rt_w_in, rt_o_gain, rt_w_out, moe_router, moe_w_gate, moe_w_up, moe_w_down):
    h = x
    for layer in range(DEPTH):
        kind = layer % N_MIXERS
        occ = layer // N_MIXERS
        u = rms_norm(h, norm_mix[layer])
        if kind == 0:
            mix = neighbourhood_attention(u, na_w_in[occ], na_q_gain[occ], na_k_gain[occ], na_rel_bias[occ], na_w_out[occ])
        elif kind == 1:
            mix = hgrn2_bidirectional(u, hg_w_in[occ], hg_lb, layer, hg_o_gain[occ], hg_w_out[occ])
        elif kind == 2:
            mix = gqa_axial(u, gq_w_in[occ], gq_q_gain[occ], gq_k_gain[occ], gq_w_out[occ])
        else:
            mix = retention_bidirectional(u, rt_w_in[occ], rt_o_gain[occ], rt_w_out[occ])
        h = h + mix
        h = h + expert_choice_ffn(rms_norm(h, norm_ffn[layer]), moe_router[layer], moe_w_gate[layer], moe_w_up[layer], moe_w_down[layer])
    return h


def setup_inputs(seed: int = 0) -> dict:
    key = jax.random.key(seed)
    keys = jax.random.split(key, 32)
    counter = [0]

    def nk():
        counter[0] += 1
        return keys[counter[0] - 1]

    def nrm(shape, fan_in):
        return jax.random.normal(nk(), shape, jnp.float32) * fan_in ** -0.5

    def gain(shape):
        return 1.0 + 0.1 * jax.random.normal(nk(), shape, jnp.float32)

    gq_in = GQA_HEADS * GQA_HEAD_DIM + 2 * GQA_KV_HEADS * GQA_HEAD_DIM
    rt_in = 2 * RET_HEADS * RET_KEY_DIM + 2 * RET_HEADS * RET_VAL_DIM
    return {
        'x_prompt': jax.random.normal(nk(), (BATCH, SEQ, D_MODEL), jnp.float32),
        'x_sample': jax.random.normal(nk(), (DEC_BATCH, DEC_SEQ, D_MODEL), jnp.float32),
        'norm_mix': gain((DEPTH, D_MODEL)),
        'norm_ffn': gain((DEPTH, D_MODEL)),
        'na_w_in': nrm((N_OCC, D_MODEL, 3 * NA_HEADS * NA_HEAD_DIM), D_MODEL),
        'na_q_gain': gain((N_OCC, NA_HEAD_DIM)),
        'na_k_gain': gain((N_OCC, NA_HEAD_DIM)),
        'na_rel_bias': 0.1 * jax.random.normal(nk(), (N_OCC, NA_HEADS, 2 * NA_WIN_ROWS - 1, 2 * NA_WIN_COLS - 1), jnp.float32),
        'na_w_out': nrm((N_OCC, NA_HEADS * NA_HEAD_DIM, D_MODEL), NA_HEADS * NA_HEAD_DIM),
        'hg_w_in': nrm((N_OCC, D_MODEL, 5 * HG_HEADS * HG_KEY_DIM), D_MODEL),
        'hg_lb': jax.random.normal(nk(), (DEPTH, HG_HEADS * HG_KEY_DIM), jnp.float32),
        'hg_o_gain': gain((N_OCC, HG_VAL_DIM)),
        'hg_w_out': nrm((N_OCC, HG_HEADS * HG_VAL_DIM, D_MODEL), HG_HEADS * HG_VAL_DIM),
        'gq_w_in': nrm((N_OCC, D_MODEL, gq_in), D_MODEL),
        'gq_q_gain': gain((N_OCC, GQA_HEAD_DIM)),
        'gq_k_gain': gain((N_OCC, GQA_HEAD_DIM)),
        'gq_w_out': nrm((N_OCC, GQA_HEADS * GQA_HEAD_DIM, D_MODEL), GQA_HEADS * GQA_HEAD_DIM),
        'rt_w_in': nrm((N_OCC, D_MODEL, rt_in), D_MODEL),
        'rt_o_gain': gain((N_OCC, RET_VAL_DIM)),
        'rt_w_out': nrm((N_OCC, RET_HEADS * RET_VAL_DIM, D_MODEL), RET_HEADS * RET_VAL_DIM),
        'moe_router': nrm((DEPTH, D_MODEL, N_EXPERTS), D_MODEL),
        'moe_w_gate': nrm((DEPTH, N_EXPERTS, D_MODEL, EXPERT_FF), D_MODEL),
        'moe_w_up': nrm((DEPTH, N_EXPERTS, D_MODEL, EXPERT_FF), D_MODEL),
        'moe_w_down': nrm((DEPTH, N_EXPERTS, EXPERT_FF, D_MODEL), EXPERT_FF),
    }


def reference(x_prompt, x_sample, norm_mix, norm_ffn, na_w_in, na_q_gain, na_k_gain, na_rel_bias, na_w_out,
              hg_w_in, hg_lb, hg_o_gain, hg_w_out, gq_w_in, gq_q_gain, gq_k_gain, gq_w_out,
              rt_w_in, rt_o_gain, rt_w_out, moe_router, moe_w_gate, moe_w_up, moe_w_down):
    layer_stack = functools.partial(
        trunk, norm_mix=norm_mix, norm_ffn=norm_ffn,
        na_w_in=na_w_in, na_q_gain=na_q_gain, na_k_gain=na_k_gain, na_rel_bias=na_rel_bias, na_w_out=na_w_out,
        hg_w_in=hg_w_in, hg_lb=hg_lb, hg_o_gain=hg_o_gain, hg_w_out=hg_w_out,
        gq_w_in=gq_w_in, gq_q_gain=gq_q_gain, gq_k_gain=gq_k_gain, gq_w_out=gq_w_out,
        rt_w_in=rt_w_in, rt_o_gain=rt_o_gain, rt_w_out=rt_w_out,
        moe_router=moe_router, moe_w_gate=moe_w_gate, moe_w_up=moe_w_up, moe_w_down=moe_w_down)
    y_prompt = layer_stack(x_prompt)
    y_sample = layer_stack(x_sample)
    return (y_prompt, y_sample)
```

```python
import functools
import math

import jax
import jax.numpy as jnp
from jax import lax
from jax.experimental import pallas as pl
from jax.experimental.pallas import tpu as pltpu

F32 = jnp.float32
BF16 = jnp.bfloat16
I32 = jnp.int32

D_MODEL = 1024
GRID_W = 64
EPS = 1e-6
ROPE_THETA = 10000.0
NEG = -1e30

NA_HEADS = 16
NA_WIN_ROWS = 8
NA_WIN_COLS = 16
HG_HEADS = 8
HG_DIM = 128
GQA_HEADS = 16
GQA_KV = 4
GQA_DH = 64
RET_HEADS = 4
RET_DK = 256
RET_DV = 512
N_EXPERTS = 16
EXPERT_FF = 2048
CAPACITY_FACTOR = 2

LANES = 128
SUBLANES = 8
VMEM_LIMIT = 52 * 1024 * 1024

NT = (((1,), (1,)), ((), ()))
TN = (((0,), (0,)), ((), ()))


def _cparams(n_axes):
    return pltpu.CompilerParams(dimension_semantics=("arbitrary",) * n_axes,
                                vmem_limit_bytes=VMEM_LIMIT)


def _sigmoid(x):
    return 1.0 / (1.0 + jnp.exp(-x))


def _proj_body(*refs, headnorm, has_cv, rope_k, tw):
    it = iter(refs)
    x_ref, g_ref, w_ref = next(it), next(it), next(it)
    cv_ref = next(it) if has_cv else None
    cos_ref = next(it) if rope_k else None
    sin_ref = next(it) if rope_k else None
    o_ref, xn_ref = next(it), next(it)

    @pl.when(pl.program_id(1) == 0)
    def _():
        x = x_ref[...]
        ms = jnp.mean(x * x, axis=-1, keepdims=True)
        xn_ref[...] = (x * lax.rsqrt(ms + EPS) * g_ref[...]).astype(BF16)

    acc = jnp.dot(xn_ref[...], w_ref[...], preferred_element_type=F32)
    if not (headnorm or has_cv or rope_k):
        o_ref[...] = acc.astype(o_ref.dtype)
        return
    lane = lax.broadcasted_iota(I32, (1, LANES), 1)
    lo = lane < 64
    for c in range(acc.shape[1] // LANES):
        cols = slice(c * LANES, (c + 1) * LANES)
        a = acc[:, cols]
        if headnorm:
            sq = a * a
            s_lo = jnp.sum(jnp.where(lo, sq, 0.0), axis=-1, keepdims=True)
            s_hi = jnp.sum(jnp.where(lo, 0.0, sq), axis=-1, keepdims=True)
            a = a * lax.rsqrt(jnp.where(lo, s_lo, s_hi) * (1.0 / 64.0) + EPS)
        if has_cv:
            a = a * cv_ref[:, cols]
        if rope_k:
            tcol = (c * LANES) % tw
            if rope_k == 64:
                partner = pltpu.roll(a, 64, 1)
            else:
                partner = jnp.where((lane & rope_k) != 0, pltpu.roll(a, rope_k, 1),
                                    pltpu.roll(a, LANES - rope_k, 1))
            a = a * cos_ref[:, tcol:tcol + LANES] + partner * sin_ref[:, tcol:tcol + LANES]
        o_ref[:, cols] = a.astype(o_ref.dtype)


def _proj(x, g, w, seq_len, out_dtype, *, headnorm=False, colvec=None, rope=None, name="proj"):
    T = x.shape[0]
    N = w.shape[1]
    tm = min(1024, seq_len)
    tn = 512 if N % 512 == 0 else 256
    assert T % tm == 0 and seq_len % tm == 0 and N % tn == 0
    nsb = seq_len // tm
    in_specs = [pl.BlockSpec((tm, D_MODEL), lambda i, j: (i, 0)),
                pl.BlockSpec((1, D_MODEL), lambda i, j: (0, 0)),
                pl.BlockSpec((D_MODEL, tn), lambda i, j: (0, j))]
    args = [x, g.reshape(1, D_MODEL).astype(F32), w]
    if colvec is not None:
        in_specs.append(pl.BlockSpec((1, tn), lambda i, j: (0, j)))
        args.append(colvec.reshape(1, N).astype(F32))
    rope_k, tw = 0, LANES
    if rope is not None:
        rope_k, cos, sin = rope
        tw = cos.shape[1]
        assert tn % tw == 0
        in_specs += [pl.BlockSpec((tm, tw), lambda i, j: (i % nsb, 0))] * 2
        args += [cos, sin]
    body = functools.partial(_proj_body, headnorm=headnorm, has_cv=colvec is not None,
                             rope_k=rope_k, tw=tw)
    return pl.pallas_call(
        body, grid=(T // tm, N // tn), in_specs=in_specs,
        out_specs=pl.BlockSpec((tm, tn), lambda i, j: (i, j)),
        out_shape=jax.ShapeDtypeStruct((T, N), out_dtype),
        scratch_shapes=[pltpu.VMEM((tm, D_MODEL), BF16)],
        compiler_params=_cparams(2), name=name)(*args)


def _group_rms(o, gain_ref, width):
    parts = []
    for c in range(o.shape[1] // width):
        a = o[:, c * width:(c + 1) * width]
        ms = jnp.mean(a * a, axis=-1, keepdims=True)
        parts.append(a * lax.rsqrt(ms + EPS) * gain_ref[...])
    return jnp.concatenate(parts, axis=1)


def _outproj_body(*refs, kind):
    it = iter(refs)
    h_ref = next(it)
    if kind == "plain":
        a = next(it)[...]
    else:
        of_ref, ob_ref, g_ref, gain_ref = next(it), next(it), next(it), next(it)
        o = of_ref[...] + ob_ref[...]
        g = g_ref[...]
        if kind == "hg":
            a = (_group_rms(o, gain_ref, HG_DIM) * _sigmoid(g)).astype(BF16)
        else:
            a = ((g * _sigmoid(g)) * _group_rms(o, gain_ref, RET_DV)).astype(BF16)
    w_ref, nf_ref, wr_ref = next(it), next(it), next(it)
    hn_ref, u_ref, aff_ref = next(it), next(it), next(it)
    hn = h_ref[...] + jnp.dot(a, w_ref[...], preferred_element_type=F32)
    hn_ref[...] = hn
    ms = jnp.mean(hn * hn, axis=-1, keepdims=True)
    u = hn * lax.rsqrt(ms + EPS) * nf_ref[...]
    u_ref[...] = u
    logits = lax.dot_general(wr_ref[...], u, NT, precision=lax.Precision.HIGHEST,
                             preferred_element_type=F32)
    m = jnp.max(logits, axis=0, keepdims=True)
    e = jnp.exp(logits - m)
    aff_ref[...] = e / jnp.sum(e, axis=0, keepdims=True)


def _outproj(h, mix_args, w_out, norm_g, w_router_t, kind, name):
    T = h.shape[0]
    tm = 256
    K = w_out.shape[0]
    row = lambda i: (i, 0)
    in_specs = [pl.BlockSpec((tm, D_MODEL), row)]
    args = [h]
    if kind == "plain":
        (a,) = mix_args
        in_specs.append(pl.BlockSpec((tm, K), row))
        args.append(a)
    else:
        o_f, o_b, g_arr, g_col, gain = mix_args
        gw = gain.shape[0]
        in_specs += [pl.BlockSpec((tm, K), row), pl.BlockSpec((tm, K), row),
                     pl.BlockSpec((tm, K), lambda i: (i, g_col)),
                     pl.BlockSpec((1, gw), lambda i: (0, 0))]
        args += [o_f, o_b, g_arr, gain.reshape(1, gw).astype(F32)]
    in_specs += [pl.BlockSpec((K, D_MODEL), lambda i: (0, 0)),
                 pl.BlockSpec((1, D_MODEL), lambda i: (0, 0)),
                 pl.BlockSpec((N_EXPERTS, D_MODEL), lambda i: (0, 0))]
    args += [w_out, norm_g.reshape(1, D_MODEL).astype(F32), w_router_t]
    return pl.pallas_call(
        functools.partial(_outproj_body, kind=kind), grid=(T // tm,), in_specs=in_specs,
        out_specs=[pl.BlockSpec((tm, D_MODEL), row), pl.BlockSpec((tm, D_MODEL), row),
                   pl.BlockSpec((N_EXPERTS, tm), lambda i: (0, i))],
        out_shape=[jax.ShapeDtypeStruct((T, D_MODEL), F32), jax.ShapeDtypeStruct((T, D_MODEL), F32),
                   jax.ShapeDtypeStruct((N_EXPERTS, T), F32)],
        compiler_params=_cparams(1), name=name)(*args)


NA_QROWS = 8


def _na_body(q_ref, kp_ref, kc_ref, kn_ref, vp_ref, vc_ref, vn_ref, bt_ref, o_ref, kbuf, vbuf, *, rows):
    rb = pl.program_id(2)
    blk = NA_QROWS * GRID_W
    for n, (kr, vr) in enumerate(((kp_ref, vp_ref), (kc_ref, vc_ref), (kn_ref, vn_ref))):
        kbuf[n * blk:(n + 1) * blk, :] = kr[...]
        vbuf[n * blk:(n + 1) * blk, :] = vr[...]
    lane = lax.broadcasted_iota(I32, (1, LANES), 1)
    lo = lane < 64
    zero = jnp.zeros((), BF16)
    for qi in range(NA_QROWS):
        r = rb * NA_QROWS + qi
        r_start = jnp.clip(r - NA_WIN_ROWS // 2, 0, rows - NA_WIN_ROWS)
        roff0 = r_start - r + (NA_WIN_ROWS - 1)
        off = pl.multiple_of((r_start - rb * NA_QROWS + NA_QROWS) * GRID_W, GRID_W)
        kw = kbuf[pl.ds(off, NA_WIN_ROWS * GRID_W), :]
        vw = vbuf[pl.ds(off, NA_WIN_ROWS * GRID_W), :]
        qp = q_ref[qi * GRID_W:(qi + 1) * GRID_W, :]
        outs = []
        for hh in range(2):
            qm = jnp.where(lo if hh == 0 else jnp.logical_not(lo), qp, zero)
            s = lax.dot_general(qm, kw, NT, preferred_element_type=F32) + bt_ref[hh, roff0]
            m = jnp.max(s, axis=-1, keepdims=True)
            p = jnp.exp(s - m)
            l = jnp.sum(p, axis=-1, keepdims=True)
            outs.append(jnp.dot(p.astype(BF16), vw, preferred_element_type=F32) / l)
        o_ref[qi * GRID_W:(qi + 1) * GRID_W, :] = jnp.where(lo, outs[0], outs[1]).astype(o_ref.dtype)


def _na_bias_table(rel_bias):
    c = jnp.arange(GRID_W)
    c_start = jnp.clip(c - NA_WIN_COLS // 2, 0, GRID_W - NA_WIN_COLS)
    kc = jnp.arange(GRID_W)
    valid = (kc[None, :] >= c_start[:, None]) & (kc[None, :] < c_start[:, None] + NA_WIN_COLS)
    coff = jnp.clip(kc[None, :] - c[:, None] + (NA_WIN_COLS - 1), 0, 2 * NA_WIN_COLS - 2)
    bm = jnp.where(valid[None, None], rel_bias[:, :, coff], NEG)
    return jnp.stack([jnp.concatenate([bm[:, r0 + i] for i in range(NA_WIN_ROWS)], axis=-1)
                      for r0 in range(NA_WIN_ROWS)], axis=1).astype(F32)


def _na_attention(qk, v, bt, bsz, seq_len):
    rows = seq_len // GRID_W
    assert rows % NA_QROWS == 0 and rows >= NA_WIN_ROWS
    nrb = rows // NA_QROWS
    blk = NA_QROWS * GRID_W
    npair = NA_HEADS // 2

    def spec(col0, shift):
        return pl.BlockSpec((blk, LANES),
                            lambda hp, b, rb: (b * nrb + jnp.clip(rb + shift, 0, nrb - 1), col0 + hp))
    in_specs = [spec(0, 0), spec(npair, -1), spec(npair, 0), spec(npair, 1),
                spec(0, -1), spec(0, 0), spec(0, 1),
                pl.BlockSpec((2, NA_WIN_ROWS, GRID_W, NA_WIN_ROWS * GRID_W), lambda hp, b, rb: (hp, 0, 0, 0))]
    return pl.pallas_call(
        functools.partial(_na_body, rows=rows), grid=(npair, bsz, nrb), in_specs=in_specs,
        out_specs=spec(0, 0),
        out_shape=jax.ShapeDtypeStruct((bsz * seq_len, D_MODEL), BF16),
        scratch_shapes=[pltpu.VMEM((3 * blk, LANES), BF16), pltpu.VMEM((3 * blk, LANES), BF16)],
        compiler_params=_cparams(3), name="na_attention")(qk, qk, qk, qk, v, v, v, bt)


HG_CHUNK = 128
HG_BLOCK = 512


def _bcast_row(x, group, r):
    C = x.shape[0]
    x3 = x.reshape(C // group, group, x.shape[1])
    return jnp.broadcast_to(x3[:, r:r + 1, :], x3.shape).reshape(x.shape)


def _hgrn_body(xq_ref, xi_ref, xf_ref, lb_ref, o_ref, st_ref, *, rev, nchunk):
    C = HG_CHUNK

    @pl.when(pl.program_id(2) == 0)
    def _():
        st_ref[...] = jnp.zeros_like(st_ref)

    lbv = lb_ref[...]
    row = lax.broadcasted_iota(I32, (C, C), 0)
    col = lax.broadcasted_iota(I32, (C, C), 1)
    tri = jnp.where((row <= col) if rev else (row >= col), 1.0, 0.0).astype(BF16)
    t_idx = lax.broadcasted_iota(I32, (C, HG_DIM), 0)
    sub = t_idx % SUBLANES

    def chunk(ci, carry):
        cc = (nchunk - 1 - ci) if rev else ci
        rws = pl.ds(pl.multiple_of(cc * C, C), C)
        q = xq_ref[rws, :] * (HG_DIM ** -0.5)
        xi = xi_ref[rws, :]
        v = xi * _sigmoid(xi)
        f = lbv + (1.0 - lbv) * _sigmoid(xf_ref[rws, :])
        k = 1.0 - f
        lf = jnp.log(f)
        lf_hi = lf.astype(BF16)
        lf_lo = (lf - lf_hi.astype(F32)).astype(BF16)
        b = (jnp.dot(tri, lf_hi, preferred_element_type=F32)
             + jnp.dot(tri, lf_lo, preferred_element_type=F32))
        vb = v.astype(BF16)
        state = st_ref[...]
        o = jnp.dot((q * jnp.exp(b)).astype(BF16), state.astype(BF16), preferred_element_type=F32)
        for s in range(SUBLANES):
            msk = (sub <= s) if rev else (sub >= s)
            e = jnp.exp(jnp.where(msk, b - _bcast_row(b, SUBLANES, s), NEG))
            sc = jnp.sum(q * _bcast_row(k, SUBLANES, s) * e, axis=-1, keepdims=True)
            o = o + sc * _bcast_row(v, SUBLANES, s)
        scores = jnp.zeros((C, C), F32)
        m = SUBLANES
        while m < C:
            right = ((t_idx // m) % 2) == 1
            rr = _bcast_row(b, 2 * m, m if rev else m - 1)
            qside = jnp.logical_not(right) if rev else right
            kside = right if rev else jnp.logical_not(right)
            qe = jnp.where(qside, q * jnp.exp(jnp.where(qside, b - rr, 0.0)), 0.0)
            ke = jnp.where(kside, k * jnp.exp(jnp.where(kside, rr - b, 0.0)), 0.0)
            sl = lax.dot_general(qe.astype(BF16), ke.astype(BF16), NT, preferred_element_type=F32)
            scores = scores + jnp.where((row // (2 * m)) == (col // (2 * m)), sl, 0.0)
            m *= 2
        o = o + jnp.dot(scores.astype(BF16), vb, preferred_element_type=F32)
        o_ref[rws, :] = o
        bl = b[0:1, :] if rev else b[C - 1:C, :]
        ke = (k * jnp.exp(bl - b)).astype(BF16)
        upd = lax.dot_general(ke, vb, TN, preferred_element_type=F32)
        decay = jnp.transpose(jnp.broadcast_to(jnp.exp(bl), (HG_DIM, HG_DIM)))
        st_ref[...] = state * decay + upd
        return carry

    lax.fori_loop(0, nchunk, chunk, 0)


def _hgrn_direction(y, lb, bsz, seq_len, rev):
    lbk = min(HG_BLOCK, seq_len)
    assert seq_len % lbk == 0 and lbk % HG_CHUNK == 0
    nb = seq_len // lbk
    fpart = 3 if rev else 2

    def spec(part):
        return pl.BlockSpec((lbk, HG_DIM),
                            lambda b, h, c: (b * nb + ((nb - 1 - c) if rev else c), part * HG_HEADS + h))
    return pl.pallas_call(
        functools.partial(_hgrn_body, rev=rev, nchunk=lbk // HG_CHUNK),
        grid=(bsz, HG_HEADS, nb),
        in_specs=[spec(0), spec(1), spec(fpart), pl.BlockSpec((1, HG_DIM), lambda b, h, c: (0, h))],
        out_specs=spec(0),
        out_shape=jax.ShapeDtypeStruct((bsz * seq_len, D_MODEL), F32),
        scratch_shapes=[pltpu.VMEM((HG_DIM, HG_DIM), F32)],
        compiler_params=_cparams(3), name="hgrn_bwd" if rev else "hgrn_fwd")(y, y, y, lb.reshape(1, D_MODEL))


def _flash_body(q_ref, k_ref, v_ref, o_ref, m_sc, l_sc, acc_sc):
    ki = pl.program_id(3)

    @pl.when(ki == 0)
    def _():
        m_sc[...] = jnp.full_like(m_sc, NEG)
        l_sc[...] = jnp.zeros_like(l_sc)
        acc_sc[...] = jnp.zeros_like(acc_sc)

    lane = lax.broadcasted_iota(I32, (1, LANES), 1)
    lo = lane < 64
    zero = jnp.zeros((), BF16)
    k = k_ref[...]
    v = v_ref[...]
    for pair in range(2):
        qp = q_ref[:, pair * LANES:(pair + 1) * LANES]
        for half in range(2):
            hd = pair * 2 + half
            qm = jnp.where(lo if half == 0 else jnp.logical_not(lo), qp, zero)
            s = lax.dot_general(qm, k, NT, preferred_element_type=F32)
            m_prev = m_sc[hd]
            m_new = jnp.maximum(m_prev, jnp.max(s, axis=-1, keepdims=True))
            alpha = jnp.exp(m_prev - m_new)
            p = jnp.exp(s - m_new)
            l_sc[hd] = alpha * l_sc[hd] + jnp.sum(p, axis=-1, keepdims=True)
            acc_sc[hd] = alpha * acc_sc[hd] + jnp.dot(p.astype(BF16), v, preferred_element_type=F32)
            m_sc[hd] = m_new

    @pl.when(ki == pl.num_programs(3) - 1)
    def _():
        for pair in range(2):
            o0 = acc_sc[2 * pair] / l_sc[2 * pair]
            o1 = acc_sc[2 * pair + 1] / l_sc[2 * pair + 1]
            o_ref[:, pair * LANES:(pair + 1) * LANES] = jnp.where(lo, o0, o1).astype(o_ref.dtype)


def _flash_gqa(qk, v, bsz, seq_len):
    tq = min(512, seq_len)
    tk = min(1024, seq_len)
    nq, nk = seq_len // tq, seq_len // tk
    gw = (GQA_HEADS // GQA_KV) * GQA_DH
    return pl.pallas_call(
        _flash_body, grid=(bsz, GQA_KV, nq, nk),
        in_specs=[pl.BlockSpec((tq, gw), lambda b, g, i, j: (b * nq + i, g)),
                  pl.BlockSpec((tk, LANES), lambda b, g, i, j: (b * nk + j, D_MODEL // LANES + g)),
                  pl.BlockSpec((tk, LANES), lambda b, g, i, j: (b * nk + j, g))],
        out_specs=pl.BlockSpec((tq, gw), lambda b, g, i, j: (b * nq + i, g)),
        out_shape=jax.ShapeDtypeStruct((bsz * seq_len, D_MODEL), BF16),
        scratch_shapes=[pltpu.VMEM((4, tq, 1), F32), pltpu.VMEM((4, tq, 1), F32),
                        pltpu.VMEM((4, tq, LANES), F32)],
        compiler_params=_cparams(4), name="flash_gqa")(qk, qk, v)


RET_CHUNK = 128
RET_BLOCK = 512


def _ret_body(q_ref, k_ref, v_ref, dm_ref, qd_ref, kd_ref, cd_ref, o_ref, st_ref, *, rev, nchunk):
    C = RET_CHUNK

    @pl.when(pl.program_id(2) == 0)
    def _():
        st_ref[...] = jnp.zeros_like(st_ref)

    def chunk(ci, carry):
        cc = (nchunk - 1 - ci) if rev else ci
        rws = pl.ds(pl.multiple_of(cc * C, C), C)
        q = q_ref[rws, :]
        k = k_ref[rws, :]
        v = v_ref[rws, :]
        state = st_ref[...]
        s = lax.dot_general(q, k, NT, preferred_element_type=F32) * dm_ref[0]
        o = jnp.dot(s.astype(BF16), v, preferred_element_type=F32)
        qs = (q.astype(F32) * qd_ref[0]).astype(BF16)
        o = o + jnp.dot(qs, state.astype(BF16), preferred_element_type=F32)
        o_ref[rws, :] = o
        ks = (k.astype(F32) * kd_ref[0]).astype(BF16)
        st_ref[...] = state * cd_ref[0] + lax.dot_general(ks, v, TN, preferred_element_type=F32)
        return carry

    lax.fori_loop(0, nchunk, chunk, 0)


def _ret_tables(rev):
    C = RET_CHUNK
    log_gamma = jnp.log1p(-jnp.exp2(-5.0 - jnp.arange(RET_HEADS, dtype=F32)))[:, None, None]
    pos = jnp.arange(C, dtype=F32)
    rel = pos[:, None] - pos[None, :]
    if rev:
        rel = -rel
        qpow, kpow = C - pos, pos
    else:
        qpow, kpow = pos + 1.0, C - 1.0 - pos
    dm = jnp.where(rel[None] >= 0, jnp.exp(rel[None] * log_gamma), 0.0)
    qd = jnp.broadcast_to(jnp.exp(qpow[None, :, None] * log_gamma), (RET_HEADS, C, RET_DK))
    kd = jnp.broadcast_to(jnp.exp(kpow[None, :, None] * log_gamma), (RET_HEADS, C, RET_DK))
    cd = jnp.broadcast_to(jnp.exp(C * log_gamma), (RET_HEADS, 1, RET_DV))
    return dm.astype(F32), qd.astype(F32), kd.astype(F32), cd.astype(F32)


def _ret_direction(qk, v, bsz, seq_len, rev):
    lbk = min(RET_BLOCK, seq_len)
    nb = seq_len // lbk
    rowblk = lambda b, h, c: b * nb + ((nb - 1 - c) if rev else c)
    tab = lambda shape: pl.BlockSpec((1,) + shape, lambda b, h, c: (h, 0, 0))
    return pl.pallas_call(
        functools.partial(_ret_body, rev=rev, nchunk=lbk // RET_CHUNK),
        grid=(bsz, RET_HEADS, nb),
        in_specs=[pl.BlockSpec((lbk, RET_DK), lambda b, h, c: (rowblk(b, h, c), h)),
                  pl.BlockSpec((lbk, RET_DK), lambda b, h, c: (rowblk(b, h, c), RET_HEADS + h)),
                  pl.BlockSpec((lbk, RET_DV), lambda b, h, c: (rowblk(b, h, c), h)),
                  tab((RET_CHUNK, RET_CHUNK)), tab((RET_CHUNK, RET_DK)), tab((RET_CHUNK, RET_DK)),
                  tab((1, RET_DV))],
        out_specs=pl.BlockSpec((lbk, RET_DV), lambda b, h, c: (rowblk(b, h, c), h)),
        out_shape=jax.ShapeDtypeStruct((bsz * seq_len, RET_HEADS * RET_DV), F32),
        scratch_shapes=[pltpu.VMEM((RET_DK, RET_DV), F32)],
        compiler_params=_cparams(3), name="ret_bwd" if rev else "ret_fwd")(qk, qk, v, *_ret_tables(rev))


MOE_FF_CHUNK = 512


def _moe_body(idx_ref, u_hbm, gate_ref, wg_ref, wu_ref, wd_ref, o_ref, xbuf, sem, *, tc):
    def issue(r, carry):
        t = idx_ref[0, 0, r]
        pltpu.make_async_copy(u_hbm.at[pl.ds(t, 1)], xbuf.at[pl.ds(r, 1)], sem).start()
        return carry

    lax.fori_loop(0, tc, issue, 0)

    def drain(r, carry):
        pltpu.make_async_copy(u_hbm.at[pl.ds(0, 1)], xbuf.at[pl.ds(r, 1)], sem).wait()
        return carry

    lax.fori_loop(0, tc, drain, 0)
    x = xbuf[...].astype(BF16)
    acc = jnp.zeros((tc, D_MODEL), F32)
    for f in range(EXPERT_FF // MOE_FF_CHUNK):
        cols = slice(f * MOE_FF_CHUNK, (f + 1) * MOE_FF_CHUNK)
        g = jnp.dot(x, wg_ref[0, :, cols], preferred_element_type=F32)
        up = jnp.dot(x, wu_ref[0, :, cols], preferred_element_type=F32)
        hid = ((g * _sigmoid(g)) * up).astype(BF16)
        acc = acc + jnp.dot(hid, wd_ref[0, cols, :], preferred_element_type=F32)
    gate = gate_ref[0]
    for c in range(D_MODEL // LANES):
        o_ref[0, :, c * LANES:(c + 1) * LANES] = acc[:, c * LANES:(c + 1) * LANES] * gate


def _moe_experts(u, idx_sorted, gate_sorted, wg, wu, wd):
    cap = idx_sorted.shape[1]
    tc = min(512, cap)
    nj = cap // tc
    idx3 = idx_sorted.reshape(N_EXPERTS * nj, 1, tc)
    gate_b = jnp.broadcast_to(gate_sorted[:, :, None], (N_EXPERTS, cap, LANES))
    wspec = lambda shape: pl.BlockSpec((1,) + shape, lambda e, j: (e, 0, 0))
    return pl.pallas_call(
        functools.partial(_moe_body, tc=tc), grid=(N_EXPERTS, nj),
        in_specs=[pl.BlockSpec((1, 1, tc), lambda e, j: (e * nj + j, 0, 0), memory_space=pltpu.SMEM),
                  pl.BlockSpec(memory_space=pl.ANY),
                  pl.BlockSpec((1, tc, LANES), lambda e, j: (e, j, 0)),
                  wspec((D_MODEL, EXPERT_FF)), wspec((D_MODEL, EXPERT_FF)), wspec((EXPERT_FF, D_MODEL))],
        out_specs=pl.BlockSpec((1, tc, D_MODEL), lambda e, j: (e, j, 0)),
        out_shape=jax.ShapeDtypeStruct((N_EXPERTS, cap, D_MODEL), F32),
        scratch_shapes=[pltpu.VMEM((tc, D_MODEL), F32), pltpu.SemaphoreType.DMA(())],
        compiler_params=_cparams(2), name="moe_experts")(idx3, u, gate_b, wg, wu, wd)


CMB_TOKENS = 256
CMB_WINDOW = 128


def _combine_body(base_ref, nr_ref, h_ref, pos_ref, ye_hbm, o_ref, buf, sem, *, cap, ntiles):
    i = pl.program_id(0)
    Wn = CMB_WINDOW
    jidx = lax.broadcasted_iota(I32, (1, Wn), 1)

    def window(e, k):
        start = (base_ref[e * ntiles + i] // SUBLANES) * SUBLANES + k * Wn
        w0 = pl.multiple_of(jnp.minimum(start, cap - Wn), SUBLANES)
        return start, w0

    def copy(e, w0):
        return pltpu.make_async_copy(ye_hbm.at[e, pl.ds(w0, Wn)], buf.at[e], sem.at[e])

    def one_round(k, acc):
        for e in range(N_EXPERTS):
            copy(e, window(e, k)[1]).start()
        sel = []
        for e in range(N_EXPERTS):
            start, w0 = window(e, k)
            p = pos_ref[:, e:e + 1]
            hit = jnp.logical_and(p - w0 == jidx, p >= start)
            sel.append(jnp.where(hit, 1.0, 0.0).astype(BF16))
        smat = jnp.concatenate(sel, axis=1)
        for e in range(N_EXPERTS):
            copy(e, window(e, k)[1]).wait()
        rows = buf[...].reshape(N_EXPERTS * Wn, D_MODEL)
        hi = rows.astype(BF16)
        lo = (rows - hi.astype(F32)).astype(BF16)
        return (acc + jnp.dot(smat, hi, preferred_element_type=F32)
                + jnp.dot(smat, lo, preferred_element_type=F32))

    o_ref[...] = lax.fori_loop(0, nr_ref[i], one_round, h_ref[...])


def _moe_combine(h, ye, pos_t, base, nrounds):
    T = h.shape[0]
    cap = ye.shape[1]
    ntiles = T // CMB_TOKENS
    assert cap >= CMB_WINDOW and cap % SUBLANES == 0
    grid_spec = pltpu.PrefetchScalarGridSpec(
        num_scalar_prefetch=2, grid=(ntiles,),
        in_specs=[pl.BlockSpec((CMB_TOKENS, D_MODEL), lambda i, b, n: (i, 0)),
                  pl.BlockSpec((CMB_TOKENS, N_EXPERTS), lambda i, b, n: (i, 0)),
                  pl.BlockSpec(memory_space=pl.ANY)],
        out_specs=pl.BlockSpec((CMB_TOKENS, D_MODEL), lambda i, b, n: (i, 0)),
        scratch_shapes=[pltpu.VMEM((N_EXPERTS, CMB_WINDOW, D_MODEL), F32),
                        pltpu.SemaphoreType.DMA((N_EXPERTS,))])
    return pl.pallas_call(
        functools.partial(_combine_body, cap=cap, ntiles=ntiles), grid_spec=grid_spec,
        out_shape=jax.ShapeDtypeStruct((T, D_MODEL), F32),
        compiler_params=_cparams(1), name="moe_combine")(base.reshape(-1), nrounds, h, pos_t, ye)


def _route(aff_t, cap):
    E, T = aff_t.shape
    _, idx = lax.top_k(aff_t, cap)
    idx_s = jnp.sort(idx, axis=-1).astype(I32)
    gate_s = jnp.take_along_axis(aff_t, idx_s, axis=1)
    pos = jnp.full((E, T), -1, I32).at[jnp.arange(E)[:, None], idx_s].set(
        jnp.broadcast_to(jnp.arange(cap, dtype=I32)[None, :], (E, cap)))
    starts = jnp.arange(T // CMB_TOKENS + 1, dtype=I32) * CMB_TOKENS
    edges = jax.vmap(lambda r: jnp.searchsorted(r, starts))(idx_s).astype(I32)
    base = edges[:, :-1]
    cnt = edges[:, 1:] - base
    need = jnp.where(cnt > 0, (base % SUBLANES + cnt + CMB_WINDOW - 1) // CMB_WINDOW, 0)
    return idx_s, gate_s, pos.T, base, jnp.max(need, axis=0).astype(I32)


def _moe_layer(h, u, aff_t, wg, wu, wd):
    T = h.shape[0]
    cap = CAPACITY_FACTOR * T // N_EXPERTS
    idx_s, gate_s, pos_t, base, nrounds = _route(aff_t, cap)
    ye = _moe_experts(u, idx_s, gate_s, wg, wu, wd)
    return _moe_combine(h, ye, pos_t, base, nrounds)


def _rope_tables(seq_len, n_freq, reps):
    t = jnp.arange(seq_len)
    inv_freq = ROPE_THETA ** (-jnp.arange(n_freq, dtype=F32) / n_freq)
    out = []
    for pos in ((t // GRID_W).astype(F32), (t % GRID_W).astype(F32)):
        ang = pos[:, None] * inv_freq[None, :]
        c = jnp.concatenate([jnp.cos(ang), jnp.cos(ang)], axis=1)
        s = jnp.concatenate([-jnp.sin(ang), jnp.sin(ang)], axis=1)
        out.append((c, s))
    cos = jnp.concatenate([out[0][0], out[1][0]], axis=1)
    sin = jnp.concatenate([out[0][1], out[1][1]], axis=1)
    return jnp.tile(cos, (1, reps)), jnp.tile(sin, (1, reps))


def _dup_heads(w, n_heads, dh):
    k = w.shape[0]
    return jnp.repeat(w.reshape(k, n_heads, 1, dh), 2, axis=2).reshape(k, n_heads * 2 * dh)


def _trunk(x, p):
    bsz, seq_len, _ = x.shape
    T = bsz * seq_len
    h = x.reshape(T, D_MODEL)
    for layer in range(4):
        nm = p["norm_mix"][layer]
        if layer == 0:
            qk = _proj(h, nm, p["na_wqk"], seq_len, BF16, headnorm=True, colvec=p["na_qk_gain"], name="na_proj_qk")
            v = _proj(h, nm, p["na_wv"], seq_len, BF16, name="na_proj_v")
            mix_args = (_na_attention(qk, v, p["na_bt"], bsz, seq_len),)
            kind, w_out = "plain", p["na_wo"]
        elif layer == 1:
            y = _proj(h, nm, p["hg_win"], seq_len, F32, name="hg_proj")
            o_f = _hgrn_direction(y, p["hg_lb"], bsz, seq_len, False)
            o_b = _hgrn_direction(y, p["hg_lb"], bsz, seq_len, True)
            mix_args = (o_f, o_b, y, 4, p["hg_o_gain"])
            kind, w_out = "hg", p["hg_wo"]
        elif layer == 2:
            cos, sin = _rope_tables(seq_len, 16, 2)
            qk = _proj(h, nm, p["gq_wqk"], seq_len, BF16, headnorm=True, colvec=p["gq_qk_gain"],
                       rope=(16, cos, sin), name="gqa_proj_qk")
            v = _proj(h, nm, p["gq_wv"], seq_len, BF16, name="gqa_proj_v")
            mix_args = (_flash_gqa(qk, v, bsz, seq_len),)
            kind, w_out = "plain", p["gq_wo"]
        else:
            cos, sin = _rope_tables(seq_len, 64, 1)
            qk = _proj(h, nm, p["rt_wqk"], seq_len, BF16, colvec=p["rt_qk_scale"],
                       rope=(64, cos, sin), name="ret_proj_qk")
            v = _proj(h, nm, p["rt_wv"], seq_len, BF16, name="ret_proj_v")
            g = _proj(h, nm, p["rt_wg"], seq_len, F32, name="ret_proj_g")
            o_f = _ret_direction(qk, v, bsz, seq_len, False)
            o_b = _ret_direction(qk, v, bsz, seq_len, True)
            mix_args = (o_f, o_b, g, 0, p["rt_o_gain"])
            kind, w_out = "rt", p["rt_wo"]
        h, u, aff_t = _outproj(h, mix_args, w_out, p["norm_ffn"][layer], p["router_t"][layer], kind,
                               name=f"outproj_{kind}")
        h = _moe_layer(h, u, aff_t, p["moe_wg"][layer], p["moe_wu"][layer], p["moe_wd"][layer])
    return h.reshape(bsz, seq_len, D_MODEL)


def kernel(x_prompt, x_sample, norm_mix, norm_ffn, na_w_in, na_q_gain, na_k_gain, na_rel_bias, na_w_out, hg_w_in, hg_lb, hg_o_gain, hg_w_out, gq_w_in, gq_q_gain, gq_k_gain, gq_w_out, rt_w_in, rt_o_gain, rt_w_out, moe_router, moe_w_gate, moe_w_up, moe_w_down):
    bf = lambda a: a.astype(BF16)
    na_w, gq_w, rt_w = na_w_in[0], gq_w_in[0], rt_w_in[0]
    lb_cum = jnp.cumsum(jax.nn.softmax(hg_lb.astype(F32), axis=0), axis=0)
    qd = GQA_HEADS * GQA_DH
    kd = GQA_KV * GQA_DH
    rq = RET_HEADS * RET_DK
    rv = RET_HEADS * RET_DV
    p = {
        "norm_mix": norm_mix, "norm_ffn": norm_ffn,
        "na_wqk": bf(na_w[:, :2 * D_MODEL]), "na_wv": bf(na_w[:, 2 * D_MODEL:]),
        "na_qk_gain": jnp.concatenate([jnp.tile(na_q_gain[0], NA_HEADS) * 64 ** -0.5,
                                       jnp.tile(na_k_gain[0], NA_HEADS)]),
        "na_bt": _na_bias_table(na_rel_bias[0]), "na_wo": bf(na_w_out[0]),
        "hg_win": bf(hg_w_in[0]), "hg_lb": lb_cum[1] - lb_cum[0], "hg_o_gain": hg_o_gain[0],
        "hg_wo": bf(hg_w_out[0]),
        "gq_wqk": bf(jnp.concatenate([gq_w[:, :qd], _dup_heads(gq_w[:, qd:qd + kd], GQA_KV, GQA_DH)], axis=1)),
        "gq_wv": bf(_dup_heads(gq_w[:, qd + kd:], GQA_KV, GQA_DH)),
        "gq_qk_gain": jnp.concatenate([jnp.tile(gq_q_gain[0], GQA_HEADS) * GQA_DH ** -0.5,
                                       jnp.tile(gq_k_gain[0], 2 * GQA_KV)]),
        "gq_wo": bf(gq_w_out[0]),
        "rt_wqk": bf(rt_w[:, :2 * rq]), "rt_wv": bf(rt_w[:, 2 * rq:2 * rq + rv]), "rt_wg": bf(rt_w[:, 2 * rq + rv:]),
        "rt_qk_scale": jnp.concatenate([jnp.full((rq,), RET_DK ** -0.5, F32), jnp.ones((rq,), F32)]),
        "rt_o_gain": rt_o_gain[0], "rt_wo": bf(rt_w_out[0]),
        "router_t": jnp.swapaxes(moe_router, 1, 2).astype(F32),
        "moe_wg": bf(moe_w_gate), "moe_wu": bf(moe_w_up), "moe_wd": bf(moe_w_down),
    }
    return (_trunk(x_prompt, p), _trunk(x_sample, p))
```

```python
import functools
import math

import jax
import jax.numpy as jnp
from jax import lax
from jax.experimental import pallas as pl
from jax.experimental.pallas import tpu as pltpu

F32 = jnp.float32
BF16 = jnp.bfloat16
I32 = jnp.int32

D_MODEL = 1024
GRID_W = 64
EPS = 1e-6
ROPE_THETA = 10000.0
NEG = -1e30
LOG2E = math.log2(math.e)

NA_HEADS = 16
NA_WIN_ROWS = 8
NA_WIN_COLS = 16
HG_HEADS = 8
HG_DIM = 128
GQA_HEADS = 16
GQA_KV = 4
GQA_DH = 64
RET_HEADS = 4
RET_DK = 256
RET_DV = 512
N_EXPERTS = 16
EXPERT_FF = 2048
CAPACITY_FACTOR = 2

LANES = 128
SUBLANES = 8
VMEM_LIMIT = 52 * 1024 * 1024

NT = (((1,), (1,)), ((), ()))
TN = (((0,), (0,)), ((), ()))


def _cparams(n_axes):
    return pltpu.CompilerParams(dimension_semantics=("arbitrary",) * n_axes,
                                vmem_limit_bytes=VMEM_LIMIT)


def _sigmoid(x):
    return 1.0 / (1.0 + jnp.exp(-x))


def _proj_body(*refs, headnorm, has_cv, rope_k, tw):
    it = iter(refs)
    x_ref, g_ref, w_ref = next(it), next(it), next(it)
    cv_ref = next(it) if has_cv else None
    cos_ref = next(it) if rope_k else None
    sin_ref = next(it) if rope_k else None
    o_ref, xn_ref = next(it), next(it)

    @pl.when(pl.program_id(1) == 0)
    def _():
        x = x_ref[...]
        ms = jnp.mean(x * x, axis=-1, keepdims=True)
        xn_ref[...] = (x * lax.rsqrt(ms + EPS) * g_ref[...]).astype(BF16)

    acc = jnp.dot(xn_ref[...], w_ref[...], preferred_element_type=F32)
    if not (headnorm or has_cv or rope_k):
        o_ref[...] = acc.astype(o_ref.dtype)
        return
    lane = lax.broadcasted_iota(I32, (1, LANES), 1)
    lo = lane < 64
    for c in range(acc.shape[1] // LANES):
        cols = slice(c * LANES, (c + 1) * LANES)
        a = acc[:, cols]
        if headnorm:
            sq = a * a
            s_lo = jnp.sum(jnp.where(lo, sq, 0.0), axis=-1, keepdims=True)
            s_hi = jnp.sum(jnp.where(lo, 0.0, sq), axis=-1, keepdims=True)
            a = a * lax.rsqrt(jnp.where(lo, s_lo, s_hi) * (1.0 / 64.0) + EPS)
        if has_cv:
            a = a * cv_ref[:, cols]
        if rope_k:
            tcol = (c * LANES) % tw
            if rope_k == 64:
                partner = pltpu.roll(a, 64, 1)
            else:
                partner = jnp.where((lane & rope_k) != 0, pltpu.roll(a, rope_k, 1),
                                    pltpu.roll(a, LANES - rope_k, 1))
            a = a * cos_ref[:, tcol:tcol + LANES] + partner * sin_ref[:, tcol:tcol + LANES]
        o_ref[:, cols] = a.astype(o_ref.dtype)


def _proj(x, g, w, seq_len, out_dtype, *, headnorm=False, colvec=None, rope=None, name="proj"):
    T = x.shape[0]
    N = w.shape[1]
    tm = min(1024, seq_len)
    tn = 512 if N % 512 == 0 else 256
    assert T % tm == 0 and seq_len % tm == 0 and N % tn == 0
    nsb = seq_len // tm
    in_specs = [pl.BlockSpec((tm, D_MODEL), lambda i, j: (i, 0)),
                pl.BlockSpec((1, D_MODEL), lambda i, j: (0, 0)),
                pl.BlockSpec((D_MODEL, tn), lambda i, j: (0, j))]
    args = [x, g.reshape(1, D_MODEL).astype(F32), w]
    if colvec is not None:
        in_specs.append(pl.BlockSpec((1, tn), lambda i, j: (0, j)))
        args.append(colvec.reshape(1, N).astype(F32))
    rope_k, tw = 0, LANES
    if rope is not None:
        rope_k, cos, sin = rope
        tw = cos.shape[1]
        assert tn % tw == 0
        in_specs += [pl.BlockSpec((tm, tw), lambda i, j: (i % nsb, 0))] * 2
        args += [cos, sin]
    body = functools.partial(_proj_body, headnorm=headnorm, has_cv=colvec is not None,
                             rope_k=rope_k, tw=tw)
    return pl.pallas_call(
        body, grid=(T // tm, N // tn), in_specs=in_specs,
        out_specs=pl.BlockSpec((tm, tn), lambda i, j: (i, j)),
        out_shape=jax.ShapeDtypeStruct((T, N), out_dtype),
        scratch_shapes=[pltpu.VMEM((tm, D_MODEL), BF16)],
        compiler_params=_cparams(2), name=name)(*args)


def _group_rms(o, gain_ref, width):
    parts = []
    for c in range(o.shape[1] // width):
        a = o[:, c * width:(c + 1) * width]
        ms = jnp.mean(a * a, axis=-1, keepdims=True)
        parts.append(a * lax.rsqrt(ms + EPS) * gain_ref[...])
    return jnp.concatenate(parts, axis=1)


def _outproj_body(*refs, kind):
    it = iter(refs)
    h_ref = next(it)
    if kind == "plain":
        a = next(it)[...]
    else:
        of_ref, ob_ref, g_ref, gain_ref = next(it), next(it), next(it), next(it)
        o = of_ref[...] + ob_ref[...]
        g = g_ref[...]
        if kind == "hg":
            a = (_group_rms(o, gain_ref, HG_DIM) * _sigmoid(g)).astype(BF16)
        else:
            a = ((g * _sigmoid(g)) * _group_rms(o, gain_ref, RET_DV)).astype(BF16)
    w_ref, nf_ref, wr_ref = next(it), next(it), next(it)
    hn_ref, u_ref, aff_ref = next(it), next(it), next(it)
    hn = h_ref[...] + jnp.dot(a, w_ref[...], preferred_element_type=F32)
    hn_ref[...] = hn
    ms = jnp.mean(hn * hn, axis=-1, keepdims=True)
    u = hn * lax.rsqrt(ms + EPS) * nf_ref[...]
    u_ref[...] = u
    logits = lax.dot_general(wr_ref[...], u, NT, precision=lax.Precision.HIGHEST,
                             preferred_element_type=F32)
    m = jnp.max(logits, axis=0, keepdims=True)
    e = jnp.exp(logits - m)
    aff_ref[...] = e / jnp.sum(e, axis=0, keepdims=True)


def _outproj(h, mix_args, w_out, norm_g, w_router_t, kind, name):
    T = h.shape[0]
    tm = 256 if kind == "rt" else 512
    K = w_out.shape[0]
    row = lambda i: (i, 0)
    in_specs = [pl.BlockSpec((tm, D_MODEL), row)]
    args = [h]
    if kind == "plain":
        (a,) = mix_args
        in_specs.append(pl.BlockSpec((tm, K), row))
        args.append(a)
    else:
        o_f, o_b, g_arr, g_col, gain = mix_args
        gw = gain.shape[0]
        in_specs += [pl.BlockSpec((tm, K), row), pl.BlockSpec((tm, K), row),
                     pl.BlockSpec((tm, K), lambda i: (i, g_col)),
                     pl.BlockSpec((1, gw), lambda i: (0, 0))]
        args += [o_f, o_b, g_arr, gain.reshape(1, gw).astype(F32)]
    in_specs += [pl.BlockSpec((K, D_MODEL), lambda i: (0, 0)),
                 pl.BlockSpec((1, D_MODEL), lambda i: (0, 0)),
                 pl.BlockSpec((N_EXPERTS, D_MODEL), lambda i: (0, 0))]
    args += [w_out, norm_g.reshape(1, D_MODEL).astype(F32), w_router_t]
    return pl.pallas_call(
        functools.partial(_outproj_body, kind=kind), grid=(T // tm,), in_specs=in_specs,
        out_specs=[pl.BlockSpec((tm, D_MODEL), row), pl.BlockSpec((tm, D_MODEL), row),
                   pl.BlockSpec((N_EXPERTS, tm), lambda i: (0, i))],
        out_shape=[jax.ShapeDtypeStruct((T, D_MODEL), F32), jax.ShapeDtypeStruct((T, D_MODEL), F32),
                   jax.ShapeDtypeStruct((N_EXPERTS, T), F32)],
        compiler_params=_cparams(1), name=name)(*args)


NA_QROWS = 8


def _na_body(q_ref, kp_ref, kc_ref, kn_ref, vp_ref, vc_ref, vn_ref, bt_ref, o_ref, kbuf, vbuf, *, rows):
    rb = pl.program_id(2)
    blk = NA_QROWS * GRID_W
    for n, (kr, vr) in enumerate(((kp_ref, vp_ref), (kc_ref, vc_ref), (kn_ref, vn_ref))):
        kbuf[n * blk:(n + 1) * blk, :] = kr[...]
        vbuf[n * blk:(n + 1) * blk, :] = vr[...]
    lane = lax.broadcasted_iota(I32, (1, LANES), 1)
    lo = lane < 64
    zero = jnp.zeros((), BF16)

    def scores(qi):
        r = rb * NA_QROWS + qi
        r_start = jnp.clip(r - NA_WIN_ROWS // 2, 0, rows - NA_WIN_ROWS)
        roff0 = r_start - r + (NA_WIN_ROWS - 1)
        off = pl.multiple_of((r_start - rb * NA_QROWS + NA_QROWS) * GRID_W, GRID_W)
        kw = kbuf[pl.ds(off, NA_WIN_ROWS * GRID_W), :]
        qp = q_ref[qi * GRID_W:(qi + 1) * GRID_W, :]
        q2 = jnp.concatenate([jnp.where(lo, qp, zero), jnp.where(lo, zero, qp)], axis=0)
        return lax.dot_general(q2, kw, NT, preferred_element_type=F32) + bt_ref[0, roff0], off

    nxt = scores(0)
    for qi in range(NA_QROWS):
        s, off = nxt
        if qi + 1 < NA_QROWS:
            nxt = scores(qi + 1)
        p = jnp.exp2(s - jnp.max(s, axis=-1, keepdims=True))
        l = jnp.sum(p, axis=-1, keepdims=True)
        o = jnp.dot(p.astype(BF16), vbuf[pl.ds(off, NA_WIN_ROWS * GRID_W), :], preferred_element_type=F32) / l
        o_ref[qi * GRID_W:(qi + 1) * GRID_W, :] = jnp.where(lo, o[:GRID_W], o[GRID_W:]).astype(o_ref.dtype)


def _na_bias_table(rel_bias):
    c = jnp.arange(GRID_W)
    c_start = jnp.clip(c - NA_WIN_COLS // 2, 0, GRID_W - NA_WIN_COLS)
    kc = jnp.arange(GRID_W)
    valid = (kc[None, :] >= c_start[:, None]) & (kc[None, :] < c_start[:, None] + NA_WIN_COLS)
    coff = jnp.clip(kc[None, :] - c[:, None] + (NA_WIN_COLS - 1), 0, 2 * NA_WIN_COLS - 2)
    bm = jnp.where(valid[None, None], rel_bias[:, :, coff] * LOG2E, NEG)
    bt = jnp.stack([jnp.concatenate([bm[:, r0 + i] for i in range(NA_WIN_ROWS)], axis=-1)
                    for r0 in range(NA_WIN_ROWS)], axis=1).astype(F32)
    bt = bt.reshape(NA_HEADS // 2, 2, NA_WIN_ROWS, GRID_W, NA_WIN_ROWS * GRID_W)
    return jnp.swapaxes(bt, 1, 2).reshape(NA_HEADS // 2, NA_WIN_ROWS, 2 * GRID_W, NA_WIN_ROWS * GRID_W)


def _na_attention(qk, v, bt, bsz, seq_len):
    rows = seq_len // GRID_W
    assert rows % NA_QROWS == 0 and rows >= NA_WIN_ROWS
    nrb = rows // NA_QROWS
    blk = NA_QROWS * GRID_W
    npair = NA_HEADS // 2

    def spec(col0, shift):
        return pl.BlockSpec((blk, LANES),
                            lambda hp, b, rb: (b * nrb + jnp.clip(rb + shift, 0, nrb - 1), col0 + hp))
    in_specs = [spec(0, 0), spec(npair, -1), spec(npair, 0), spec(npair, 1),
                spec(0, -1), spec(0, 0), spec(0, 1),
                pl.BlockSpec((1, NA_WIN_ROWS, 2 * GRID_W, NA_WIN_ROWS * GRID_W), lambda hp, b, rb: (hp, 0, 0, 0))]
    return pl.pallas_call(
        functools.partial(_na_body, rows=rows), grid=(npair, bsz, nrb), in_specs=in_specs,
        out_specs=spec(0, 0),
        out_shape=jax.ShapeDtypeStruct((bsz * seq_len, D_MODEL), BF16),
        scratch_shapes=[pltpu.VMEM((3 * blk, LANES), BF16), pltpu.VMEM((3 * blk, LANES), BF16)],
        compiler_params=_cparams(3), name="na_attention")(qk, qk, qk, qk, v, v, v, bt)


HG_CHUNK = 128
HG_BLOCK = 512


def _bcast_row(x, group, r):
    C = x.shape[0]
    x3 = x.reshape(C // group, group, x.shape[1])
    return jnp.broadcast_to(x3[:, r:r + 1, :], x3.shape).reshape(x.shape)


def _hgrn_body(xq_ref, xi_ref, xf_ref, lb_ref, o_ref, st_ref, *, rev, nchunk):
    C = HG_CHUNK

    @pl.when(pl.program_id(2) == 0)
    def _():
        st_ref[...] = jnp.zeros_like(st_ref)

    lbv = lb_ref[...]
    row = lax.broadcasted_iota(I32, (C, C), 0)
    col = lax.broadcasted_iota(I32, (C, C), 1)
    tri = jnp.where((row <= col) if rev else (row >= col), 1.0, 0.0).astype(BF16)
    t_idx = lax.broadcasted_iota(I32, (C, HG_DIM), 0)
    sub = t_idx % SUBLANES

    def chunk(ci, carry):
        cc = (nchunk - 1 - ci) if rev else ci
        rws = pl.ds(pl.multiple_of(cc * C, C), C)
        q = xq_ref[rws, :] * (HG_DIM ** -0.5)
        xi = xi_ref[rws, :]
        v = xi * _sigmoid(xi)
        f = lbv + (1.0 - lbv) * _sigmoid(xf_ref[rws, :])
        k = 1.0 - f
        lf = jnp.log(f)
        lf_hi = lf.astype(BF16)
        lf_lo = (lf - lf_hi.astype(F32)).astype(BF16)
        b = (jnp.dot(tri, lf_hi, preferred_element_type=F32)
             + jnp.dot(tri, lf_lo, preferred_element_type=F32))
        vb = v.astype(BF16)
        state = st_ref[...]
        o = jnp.dot((q * jnp.exp(b)).astype(BF16), state.astype(BF16), preferred_element_type=F32)
        for s in range(SUBLANES):
            msk = (sub <= s) if rev else (sub >= s)
            e = jnp.exp(jnp.where(msk, b - _bcast_row(b, SUBLANES, s), NEG))
            sc = jnp.sum(q * _bcast_row(k, SUBLANES, s) * e, axis=-1, keepdims=True)
            o = o + sc * _bcast_row(v, SUBLANES, s)
        scores = jnp.zeros((C, C), F32)
        m = SUBLANES
        while m < C:
            right = ((t_idx // m) % 2) == 1
            rr = _bcast_row(b, 2 * m, m if rev else m - 1)
            qside = jnp.logical_not(right) if rev else right
            kside = right if rev else jnp.logical_not(right)
            qe = jnp.where(qside, q * jnp.exp(jnp.where(qside, b - rr, 0.0)), 0.0)
            ke = jnp.where(kside, k * jnp.exp(jnp.where(kside, rr - b, 0.0)), 0.0)
            sl = lax.dot_general(qe.astype(BF16), ke.astype(BF16), NT, preferred_element_type=F32)
            scores = scores + jnp.where((row // (2 * m)) == (col // (2 * m)), sl, 0.0)
            m *= 2
        o = o + jnp.dot(scores.astype(BF16), vb, preferred_element_type=F32)
        o_ref[rws, :] = o
        bl = b[0:1, :] if rev else b[C - 1:C, :]
        ke = (k * jnp.exp(bl - b)).astype(BF16)
        upd = lax.dot_general(ke, vb, TN, preferred_element_type=F32)
        decay = jnp.transpose(jnp.broadcast_to(jnp.exp(bl), (HG_DIM, HG_DIM)))
        st_ref[...] = state * decay + upd
        return carry

    lax.fori_loop(0, nchunk, chunk, 0)


def _hgrn_direction(y, lb, bsz, seq_len, rev):
    lbk = min(HG_BLOCK, seq_len)
    assert seq_len % lbk == 0 and lbk % HG_CHUNK == 0
    nb = seq_len // lbk
    fpart = 3 if rev else 2

    def spec(part):
        return pl.BlockSpec((lbk, HG_DIM),
                            lambda b, h, c: (b * nb + ((nb - 1 - c) if rev else c), part * HG_HEADS + h))
    return pl.pallas_call(
        functools.partial(_hgrn_body, rev=rev, nchunk=lbk // HG_CHUNK),
        grid=(bsz, HG_HEADS, nb),
        in_specs=[spec(0), spec(1), spec(fpart), pl.BlockSpec((1, HG_DIM), lambda b, h, c: (0, h))],
        out_specs=spec(0),
        out_shape=jax.ShapeDtypeStruct((bsz * seq_len, D_MODEL), F32),
        scratch_shapes=[pltpu.VMEM((HG_DIM, HG_DIM), F32)],
        compiler_params=_cparams(3), name="hgrn_bwd" if rev else "hgrn_fwd")(y, y, y, lb.reshape(1, D_MODEL))


def _flash_body(q_ref, k_ref, v_ref, o_ref, m_sc, acc_sc):
    ki = pl.program_id(3)

    @pl.when(ki == 0)
    def _():
        m_sc[...] = jnp.full_like(m_sc, NEG)
        acc_sc[...] = jnp.zeros_like(acc_sc)

    lane = lax.broadcasted_iota(I32, (1, LANES), 1)
    lo = lane < 64
    zero = jnp.zeros((), BF16)
    k = k_ref[...]
    v1 = jnp.where(lo, v_ref[...], jnp.ones((), BF16))

    def scores(hd):
        qp = q_ref[:, (hd // 2) * LANES:(hd // 2 + 1) * LANES]
        qm = jnp.where(lo if hd % 2 == 0 else jnp.logical_not(lo), qp, zero)
        return lax.dot_general(qm, k, NT, preferred_element_type=F32)

    n_heads = GQA_HEADS // GQA_KV
    s_next = scores(0)
    for hd in range(n_heads):
        s = s_next
        if hd + 1 < n_heads:
            s_next = scores(hd + 1)
        m_prev = m_sc[hd]
        m_new = jnp.maximum(m_prev, jnp.max(s, axis=-1, keepdims=True))
        p = jnp.exp2(s - m_new)
        acc_sc[hd] = (jnp.exp2(m_prev - m_new) * acc_sc[hd]
                      + jnp.dot(p.astype(BF16), v1, preferred_element_type=F32))
        m_sc[hd] = m_new

    @pl.when(ki == pl.num_programs(3) - 1)
    def _():
        for pair in range(2):
            a0 = acc_sc[2 * pair]
            a1 = pltpu.roll(acc_sc[2 * pair + 1], 64, 1)
            o_ref[:, pair * LANES:(pair + 1) * LANES] = jnp.where(
                lo, a0 / a0[:, 64:65], a1 / a1[:, 0:1]).astype(o_ref.dtype)


def _flash_gqa(qk, v, bsz, seq_len):
    tq = min(512, seq_len)
    tk = min(1024, seq_len)
    nq, nk = seq_len // tq, seq_len // tk
    gw = (GQA_HEADS // GQA_KV) * GQA_DH
    return pl.pallas_call(
        _flash_body, grid=(bsz, GQA_KV, nq, nk),
        in_specs=[pl.BlockSpec((tq, gw), lambda b, g, i, j: (b * nq + i, g)),
                  pl.BlockSpec((tk, LANES), lambda b, g, i, j: (b * nk + j, D_MODEL // LANES + g)),
                  pl.BlockSpec((tk, LANES), lambda b, g, i, j: (b * nk + j, g))],
        out_specs=pl.BlockSpec((tq, gw), lambda b, g, i, j: (b * nq + i, g)),
        out_shape=jax.ShapeDtypeStruct((bsz * seq_len, D_MODEL), BF16),
        scratch_shapes=[pltpu.VMEM((4, tq, 1), F32), pltpu.VMEM((4, tq, LANES), F32)],
        compiler_params=_cparams(4), name="flash_gqa")(qk, qk, v)


RET_CHUNK = 128
RET_BLOCK = 512


def _ret_body(q_ref, k_ref, v_ref, dm_ref, qd_ref, kd_ref, cd_ref, o_ref, st_ref, *, rev, nchunk):
    C = RET_CHUNK

    @pl.when(pl.program_id(2) == 0)
    def _():
        st_ref[...] = jnp.zeros_like(st_ref)

    def chunk(ci, carry):
        cc = (nchunk - 1 - ci) if rev else ci
        rws = pl.ds(pl.multiple_of(cc * C, C), C)
        q = q_ref[rws, :]
        k = k_ref[rws, :]
        v = v_ref[rws, :]
        state = st_ref[...]
        s = lax.dot_general(q, k, NT, preferred_element_type=F32) * dm_ref[0]
        o = jnp.dot(s.astype(BF16), v, preferred_element_type=F32)
        qs = (q.astype(F32) * qd_ref[0]).astype(BF16)
        o = o + jnp.dot(qs, state.astype(BF16), preferred_element_type=F32)
        o_ref[rws, :] = o
        ks = (k.astype(F32) * kd_ref[0]).astype(BF16)
        st_ref[...] = state * cd_ref[0] + lax.dot_general(ks, v, TN, preferred_element_type=F32)
        return carry

    lax.fori_loop(0, nchunk, chunk, 0)


def _ret_tables(rev):
    C = RET_CHUNK
    log_gamma = jnp.log1p(-jnp.exp2(-5.0 - jnp.arange(RET_HEADS, dtype=F32)))[:, None, None]
    pos = jnp.arange(C, dtype=F32)
    rel = pos[:, None] - pos[None, :]
    if rev:
        rel = -rel
        qpow, kpow = C - pos, pos
    else:
        qpow, kpow = pos + 1.0, C - 1.0 - pos
    dm = jnp.where(rel[None] >= 0, jnp.exp(rel[None] * log_gamma), 0.0)
    qd = jnp.broadcast_to(jnp.exp(qpow[None, :, None] * log_gamma), (RET_HEADS, C, RET_DK))
    kd = jnp.broadcast_to(jnp.exp(kpow[None, :, None] * log_gamma), (RET_HEADS, C, RET_DK))
    cd = jnp.broadcast_to(jnp.exp(C * log_gamma), (RET_HEADS, 1, RET_DV))
    return dm.astype(F32), qd.astype(F32), kd.astype(F32), cd.astype(F32)


def _ret_direction(qk, v, bsz, seq_len, rev):
    lbk = min(RET_BLOCK, seq_len)
    nb = seq_len // lbk
    rowblk = lambda b, h, c: b * nb + ((nb - 1 - c) if rev else c)
    tab = lambda shape: pl.BlockSpec((1,) + shape, lambda b, h, c: (h, 0, 0))
    return pl.pallas_call(
        functools.partial(_ret_body, rev=rev, nchunk=lbk // RET_CHUNK),
        grid=(bsz, RET_HEADS, nb),
        in_specs=[pl.BlockSpec((lbk, RET_DK), lambda b, h, c: (rowblk(b, h, c), h)),
                  pl.BlockSpec((lbk, RET_DK), lambda b, h, c: (rowblk(b, h, c), RET_HEADS + h)),
                  pl.BlockSpec((lbk, RET_DV), lambda b, h, c: (rowblk(b, h, c), h)),
                  tab((RET_CHUNK, RET_CHUNK)), tab((RET_CHUNK, RET_DK)), tab((RET_CHUNK, RET_DK)),
                  tab((1, RET_DV))],
        out_specs=pl.BlockSpec((lbk, RET_DV), lambda b, h, c: (rowblk(b, h, c), h)),
        out_shape=jax.ShapeDtypeStruct((bsz * seq_len, RET_HEADS * RET_DV), F32),
        scratch_shapes=[pltpu.VMEM((RET_DK, RET_DV), F32)],
        compiler_params=_cparams(3), name="ret_bwd" if rev else "ret_fwd")(qk, qk, v, *_ret_tables(rev))


MOE_FF_CHUNK = 512


def _moe_body(idx_ref, nxt_ref, u_hbm, gate_ref, wg_ref, wu_ref, wd_ref, hi_ref, lo_ref, xbuf, sem, *, tc):
    nsteps = pl.num_programs(0) * pl.num_programs(1)
    step = pl.program_id(0) * pl.num_programs(1) + pl.program_id(1)
    slot = step % 2

    def row_copy(token, r, s):
        return pltpu.make_async_copy(u_hbm.at[pl.ds(token, 1)], xbuf.at[s, pl.ds(r, 1)], sem.at[s])

    def wait_rows(s):
        for _ in range(tc):
            row_copy(0, 0, s).wait()

    @pl.when(step == 0)
    def _():
        def issue(r, carry):
            row_copy(idx_ref[0, 0, r], r, 0).start()
            return carry
        lax.fori_loop(0, tc, issue, 0)

    wait_rows(slot)
    x = xbuf[slot].astype(BF16)
    for r in range(tc):
        row_copy(nxt_ref[0, 0, r], r, 1 - slot).start()
    acc = jnp.zeros((tc, D_MODEL), F32)
    for f in range(EXPERT_FF // MOE_FF_CHUNK):
        cols = slice(f * MOE_FF_CHUNK, (f + 1) * MOE_FF_CHUNK)
        g = jnp.dot(x, wg_ref[0, :, cols], preferred_element_type=F32)
        up = jnp.dot(x, wu_ref[0, :, cols], preferred_element_type=F32)
        hid = ((g * _sigmoid(g)) * up).astype(BF16)
        acc = acc + jnp.dot(hid, wd_ref[0, cols, :], preferred_element_type=F32)
    gate = gate_ref[0]
    for c in range(D_MODEL // LANES):
        cols = slice(c * LANES, (c + 1) * LANES)
        ye = acc[:, cols] * gate
        hi = ye.astype(BF16)
        hi_ref[0, :, cols] = hi
        lo_ref[0, :, cols] = (ye - hi.astype(F32)).astype(BF16)

    @pl.when(step == nsteps - 1)
    def _():
        wait_rows(1 - slot)


def _moe_experts(u, idx_sorted, gate_sorted, wg, wu, wd):
    cap = idx_sorted.shape[1]
    tc = min(512, cap)
    nj = cap // tc
    nsteps = N_EXPERTS * nj
    idx3 = idx_sorted.reshape(nsteps, 1, tc)
    gate_b = jnp.broadcast_to(gate_sorted[:, :, None], (N_EXPERTS, cap, LANES))
    wspec = lambda shape: pl.BlockSpec((1,) + shape, lambda e, j: (e, 0, 0))
    ospec = pl.BlockSpec((1, tc, D_MODEL), lambda e, j: (e, j, 0))
    oshape = jax.ShapeDtypeStruct((N_EXPERTS, cap, D_MODEL), BF16)
    return pl.pallas_call(
        functools.partial(_moe_body, tc=tc), grid=(N_EXPERTS, nj),
        in_specs=[pl.BlockSpec((1, 1, tc), lambda e, j: (e * nj + j, 0, 0), memory_space=pltpu.SMEM),
                  pl.BlockSpec((1, 1, tc), lambda e, j: (jnp.minimum(e * nj + j + 1, nsteps - 1), 0, 0),
                               memory_space=pltpu.SMEM),
                  pl.BlockSpec(memory_space=pl.ANY),
                  pl.BlockSpec((1, tc, LANES), lambda e, j: (e, j, 0)),
                  wspec((D_MODEL, EXPERT_FF)), wspec((D_MODEL, EXPERT_FF)), wspec((EXPERT_FF, D_MODEL))],
        out_specs=[ospec, ospec], out_shape=[oshape, oshape],
        scratch_shapes=[pltpu.VMEM((2, tc, D_MODEL), F32), pltpu.SemaphoreType.DMA((2,))],
        compiler_params=_cparams(2), name="moe_experts")(idx3, idx3, u, gate_b, wg, wu, wd)


CMB_TOKENS = 256
CMB_WINDOW = LANES // 2
CMB_ALIGN = 2 * SUBLANES


def _combine_body(base_ref, nr_ref, h_ref, pos_ref, hi_hbm, lo_hbm, o_ref, buf, xbuf, sem, xsem, *, cap, ntiles):
    i = pl.program_id(0)
    slot = i % 2
    Wn = CMB_WINDOW
    lane = lax.broadcasted_iota(I32, (1, LANES), 1)
    lo_half = lane < Wn

    def window(tile, e, k):
        start = (base_ref[e * ntiles + tile] // CMB_ALIGN) * CMB_ALIGN + k * Wn
        w0 = pl.multiple_of(jnp.minimum(start, cap - Wn), CMB_ALIGN)
        return start, w0

    def copies(tile, k, dst, dsem):
        out = []
        for e in range(N_EXPERTS):
            w0 = window(tile, e, k)[1]
            out.append(pltpu.make_async_copy(hi_hbm.at[e, pl.ds(w0, Wn)], dst.at[0, e], dsem))
            out.append(pltpu.make_async_copy(lo_hbm.at[e, pl.ds(w0, Wn)], dst.at[1, e], dsem))
        return out

    def one_hot(k, check_start):
        parts = []
        for e in range(0, N_EXPERTS, 2):
            s0, w0 = window(i, e, k)
            s1, w1 = window(i, e + 1, k)
            p0 = pos_ref[:, e:e + 1]
            p1 = pos_ref[:, e + 1:e + 2]
            hit = jnp.where(lo_half, p0 - w0, p1 - w1 + Wn) == lane
            if check_start:
                hit = jnp.logical_and(hit, jnp.where(lo_half, p0 - s0, p1 - s1) >= 0)
            parts.append(jnp.where(hit, 1.0, 0.0).astype(BF16))
        return jnp.concatenate(parts, axis=1)

    def placed(smat, src):
        return (jnp.dot(smat, src[0].reshape(N_EXPERTS * Wn, D_MODEL), preferred_element_type=F32)
                + jnp.dot(smat, src[1].reshape(N_EXPERTS * Wn, D_MODEL), preferred_element_type=F32))

    @pl.when(i == 0)
    def _():
        for c in copies(0, 0, buf.at[0], sem.at[0]):
            c.start()

    @pl.when(i + 1 < ntiles)
    def _():
        for c in copies(i + 1, 0, buf.at[1 - slot], sem.at[1 - slot]):
            c.start()

    smat = one_hot(0, False)
    for c in copies(i, 0, buf.at[slot], sem.at[slot]):
        c.wait()
    acc = h_ref[...] + placed(smat, buf[slot])

    def extra_round(k, acc):
        for c in copies(i, k, xbuf, xsem):
            c.start()
        smat = one_hot(k, True)
        for c in copies(i, k, xbuf, xsem):
            c.wait()
        return acc + placed(smat, xbuf[...])

    o_ref[...] = lax.fori_loop(1, nr_ref[i], extra_round, acc)


def _moe_combine(h, ye_hi, ye_lo, pos_t, base, nrounds):
    T = h.shape[0]
    cap = ye_hi.shape[1]
    ntiles = T // CMB_TOKENS
    assert cap >= CMB_WINDOW and cap % CMB_ALIGN == 0
    wshape = (2, N_EXPERTS, CMB_WINDOW, D_MODEL)
    grid_spec = pltpu.PrefetchScalarGridSpec(
        num_scalar_prefetch=2, grid=(ntiles,),
        in_specs=[pl.BlockSpec((CMB_TOKENS, D_MODEL), lambda i, b, n: (i, 0)),
                  pl.BlockSpec((CMB_TOKENS, N_EXPERTS), lambda i, b, n: (i, 0)),
                  pl.BlockSpec(memory_space=pl.ANY), pl.BlockSpec(memory_space=pl.ANY)],
        out_specs=pl.BlockSpec((CMB_TOKENS, D_MODEL), lambda i, b, n: (i, 0)),
        scratch_shapes=[pltpu.VMEM((2,) + wshape, BF16), pltpu.VMEM(wshape, BF16),
                        pltpu.SemaphoreType.DMA((2,)), pltpu.SemaphoreType.DMA(())])
    return pl.pallas_call(
        functools.partial(_combine_body, cap=cap, ntiles=ntiles), grid_spec=grid_spec,
        out_shape=jax.ShapeDtypeStruct((T, D_MODEL), F32),
        compiler_params=_cparams(1), name="moe_combine")(base.reshape(-1), nrounds, h, pos_t, ye_hi, ye_lo)


def _route(aff_t, cap):
    E, T = aff_t.shape
    ntiles = T // CMB_TOKENS
    _, idx = lax.top_k(aff_t, cap)
    idx_s = jnp.sort(idx, axis=-1).astype(I32)
    gate_s = jnp.take_along_axis(aff_t, idx_s, axis=1)
    pos = jnp.full((E, T), -1, I32).at[jnp.arange(E)[:, None], idx_s].set(
        jnp.broadcast_to(jnp.arange(cap, dtype=I32)[None, :], (E, cap)))
    cnt = jnp.sum((pos >= 0).reshape(E, ntiles, CMB_TOKENS), axis=-1, dtype=I32)
    base = jnp.cumsum(cnt, axis=1, dtype=I32) - cnt
    need = jnp.where(cnt > 0, (base % CMB_ALIGN + cnt + CMB_WINDOW - 1) // CMB_WINDOW, 0)
    return idx_s, gate_s, pos.T, base, jnp.max(need, axis=0).astype(I32)


def _moe_layer(h, u, aff_t, wg, wu, wd):
    T = h.shape[0]
    cap = CAPACITY_FACTOR * T // N_EXPERTS
    idx_s, gate_s, pos_t, base, nrounds = _route(aff_t, cap)
    ye_hi, ye_lo = _moe_experts(u, idx_s, gate_s, wg, wu, wd)
    return _moe_combine(h, ye_hi, ye_lo, pos_t, base, nrounds)


def _rope_tables(seq_len, n_freq, reps):
    t = jnp.arange(seq_len)
    inv_freq = ROPE_THETA ** (-jnp.arange(n_freq, dtype=F32) / n_freq)
    out = []
    for pos in ((t // GRID_W).astype(F32), (t % GRID_W).astype(F32)):
        ang = pos[:, None] * inv_freq[None, :]
        c = jnp.concatenate([jnp.cos(ang), jnp.cos(ang)], axis=1)
        s = jnp.concatenate([-jnp.sin(ang), jnp.sin(ang)], axis=1)
        out.append((c, s))
    cos = jnp.concatenate([out[0][0], out[1][0]], axis=1)
    sin = jnp.concatenate([out[0][1], out[1][1]], axis=1)
    return jnp.tile(cos, (1, reps)), jnp.tile(sin, (1, reps))


def _dup_heads(w, n_heads, dh):
    k = w.shape[0]
    return jnp.repeat(w.reshape(k, n_heads, 1, dh), 2, axis=2).reshape(k, n_heads * 2 * dh)


def _trunk(x, p):
    bsz, seq_len, _ = x.shape
    T = bsz * seq_len
    h = x.reshape(T, D_MODEL)
    for layer in range(4):
        nm = p["norm_mix"][layer]
        if layer == 0:
            qk = _proj(h, nm, p["na_wqk"], seq_len, BF16, headnorm=True, colvec=p["na_qk_gain"], name="na_proj_qk")
            v = _proj(h, nm, p["na_wv"], seq_len, BF16, name="na_proj_v")
            mix_args = (_na_attention(qk, v, p["na_bt"], bsz, seq_len),)
            kind, w_out = "plain", p["na_wo"]
        elif layer == 1:
            y = _proj(h, nm, p["hg_win"], seq_len, F32, name="hg_proj")
            o_f = _hgrn_direction(y, p["hg_lb"], bsz, seq_len, False)
            o_b = _hgrn_direction(y, p["hg_lb"], bsz, seq_len, True)
            mix_args = (o_f, o_b, y, 4, p["hg_o_gain"])
            kind, w_out = "hg", p["hg_wo"]
        elif layer == 2:
            cos, sin = _rope_tables(seq_len, 16, 2)
            qk = _proj(h, nm, p["gq_wqk"], seq_len, BF16, headnorm=True, colvec=p["gq_qk_gain"],
                       rope=(16, cos, sin), name="gqa_proj_qk")
            v = _proj(h, nm, p["gq_wv"], seq_len, BF16, name="gqa_proj_v")
            mix_args = (_flash_gqa(qk, v, bsz, seq_len),)
            kind, w_out = "plain", p["gq_wo"]
        else:
            cos, sin = _rope_tables(seq_len, 64, 1)
            qk = _proj(h, nm, p["rt_wqk"], seq_len, BF16, colvec=p["rt_qk_scale"],
                       rope=(64, cos, sin), name="ret_proj_qk")
            v = _proj(h, nm, p["rt_wv"], seq_len, BF16, name="ret_proj_v")
            g = _proj(h, nm, p["rt_wg"], seq_len, F32, name="ret_proj_g")
            o_f = _ret_direction(qk, v, bsz, seq_len, False)
            o_b = _ret_direction(qk, v, bsz, seq_len, True)
            mix_args = (o_f, o_b, g, 0, p["rt_o_gain"])
            kind, w_out = "rt", p["rt_wo"]
        h, u, aff_t = _outproj(h, mix_args, w_out, p["norm_ffn"][layer], p["router_t"][layer], kind,
                               name=f"outproj_{kind}")
        h = _moe_layer(h, u, aff_t, p["moe_wg"][layer], p["moe_wu"][layer], p["moe_wd"][layer])
    return h.reshape(bsz, seq_len, D_MODEL)


def kernel(x_prompt, x_sample, norm_mix, norm_ffn, na_w_in, na_q_gain, na_k_gain, na_rel_bias, na_w_out, hg_w_in, hg_lb, hg_o_gain, hg_w_out, gq_w_in, gq_q_gain, gq_k_gain, gq_w_out, rt_w_in, rt_o_gain, rt_w_out, moe_router, moe_w_gate, moe_w_up, moe_w_down):
    bf = lambda a: a.astype(BF16)
    na_w, gq_w, rt_w = na_w_in[0], gq_w_in[0], rt_w_in[0]
    lb_cum = jnp.cumsum(jax.nn.softmax(hg_lb.astype(F32), axis=0), axis=0)
    qd = GQA_HEADS * GQA_DH
    kd = GQA_KV * GQA_DH
    rq = RET_HEADS * RET_DK
    rv = RET_HEADS * RET_DV
    p = {
        "norm_mix": norm_mix, "norm_ffn": norm_ffn,
        "na_wqk": bf(na_w[:, :2 * D_MODEL]), "na_wv": bf(na_w[:, 2 * D_MODEL:]),
        "na_qk_gain": jnp.concatenate([jnp.tile(na_q_gain[0], NA_HEADS) * (64 ** -0.5 * LOG2E),
                                       jnp.tile(na_k_gain[0], NA_HEADS)]),
        "na_bt": _na_bias_table(na_rel_bias[0]), "na_wo": bf(na_w_out[0]),
        "hg_win": bf(hg_w_in[0]), "hg_lb": lb_cum[1] - lb_cum[0], "hg_o_gain": hg_o_gain[0],
        "hg_wo": bf(hg_w_out[0]),
        "gq_wqk": bf(jnp.concatenate([gq_w[:, :qd], _dup_heads(gq_w[:, qd:qd + kd], GQA_KV, GQA_DH)], axis=1)),
        "gq_wv": bf(_dup_heads(gq_w[:, qd + kd:], GQA_KV, GQA_DH)),
        "gq_qk_gain": jnp.concatenate([jnp.tile(gq_q_gain[0], GQA_HEADS) * (GQA_DH ** -0.5 * LOG2E),
                                       jnp.tile(gq_k_gain[0], 2 * GQA_KV)]),
        "gq_wo": bf(gq_w_out[0]),
        "rt_wqk": bf(rt_w[:, :2 * rq]), "rt_wv": bf(rt_w[:, 2 * rq:2 * rq + rv]), "rt_wg": bf(rt_w[:, 2 * rq + rv:]),
        "rt_qk_scale": jnp.concatenate([jnp.full((rq,), RET_DK ** -0.5, F32), jnp.ones((rq,), F32)]),
        "rt_o_gain": rt_o_gain[0], "rt_wo": bf(rt_w_out[0]),
        "router_t": jnp.swapaxes(moe_router, 1, 2).astype(F32),
        "moe_wg": bf(moe_w_gate), "moe_wu": bf(moe_w_up), "moe_wd": bf(moe_w_down),
    }
    return (_trunk(x_prompt, p), _trunk(x_sample, p))
```

```python
import functools
import math

import jax
import jax.numpy as jnp
from jax import lax
from jax.experimental import pallas as pl
from jax.experimental.pallas import tpu as pltpu

F32 = jnp.float32
BF16 = jnp.bfloat16
I32 = jnp.int32

D_MODEL = 1024
GRID_W = 64
EPS = 1e-6
ROPE_THETA = 10000.0
NEG = -1e30
LOG2E = math.log2(math.e)

NA_HEADS = 16
NA_WIN_ROWS = 8
NA_WIN_COLS = 16
HG_HEADS = 8
HG_DIM = 128
GQA_HEADS = 16
GQA_KV = 4
GQA_DH = 64
RET_HEADS = 4
RET_DK = 256
RET_DV = 512
N_EXPERTS = 16
EXPERT_FF = 2048
CAPACITY_FACTOR = 2

LANES = 128
SUBLANES = 8
VMEM_LIMIT = 52 * 1024 * 1024

NT = (((1,), (1,)), ((), ()))
TN = (((0,), (0,)), ((), ()))


def _cparams(n_axes):
    return pltpu.CompilerParams(dimension_semantics=("arbitrary",) * n_axes,
                                vmem_limit_bytes=VMEM_LIMIT)


def _sigmoid(x):
    return 1.0 / (1.0 + jnp.exp(-x))


def _proj_body(*refs, headnorm, has_cv, rope_k, tw):
    it = iter(refs)
    x_ref, g_ref, w_ref = next(it), next(it), next(it)
    cv_ref = next(it) if has_cv else None
    cos_ref = next(it) if rope_k else None
    sin_ref = next(it) if rope_k else None
    o_ref, xn_ref = next(it), next(it)

    @pl.when(pl.program_id(1) == 0)
    def _():
        x = x_ref[...]
        ms = jnp.mean(x * x, axis=-1, keepdims=True)
        xn_ref[...] = (x * lax.rsqrt(ms + EPS) * g_ref[...]).astype(BF16)

    acc = jnp.dot(xn_ref[...], w_ref[...], preferred_element_type=F32)
    if not (headnorm or has_cv or rope_k):
        o_ref[...] = acc.astype(o_ref.dtype)
        return
    lane = lax.broadcasted_iota(I32, (1, LANES), 1)
    lo = lane < 64
    for c in range(acc.shape[1] // LANES):
        cols = slice(c * LANES, (c + 1) * LANES)
        a = acc[:, cols]
        if headnorm:
            sq = a * a
            s_lo = jnp.sum(jnp.where(lo, sq, 0.0), axis=-1, keepdims=True)
            s_hi = jnp.sum(jnp.where(lo, 0.0, sq), axis=-1, keepdims=True)
            a = a * lax.rsqrt(jnp.where(lo, s_lo, s_hi) * (1.0 / 64.0) + EPS)
        if has_cv:
            a = a * cv_ref[:, cols]
        if rope_k:
            tcol = (c * LANES) % tw
            if rope_k == 64:
                partner = pltpu.roll(a, 64, 1)
            else:
                partner = jnp.where((lane & rope_k) != 0, pltpu.roll(a, rope_k, 1),
                                    pltpu.roll(a, LANES - rope_k, 1))
            a = a * cos_ref[:, tcol:tcol + LANES] + partner * sin_ref[:, tcol:tcol + LANES]
        o_ref[:, cols] = a.astype(o_ref.dtype)


def _proj(x, g, w, seq_len, out_dtype, *, headnorm=False, colvec=None, rope=None, name="proj"):
    T = x.shape[0]
    N = w.shape[1]
    tm = min(1024, seq_len)
    tn = 512 if N % 512 == 0 else 256
    assert T % tm == 0 and seq_len % tm == 0 and N % tn == 0
    nsb = seq_len // tm
    in_specs = [pl.BlockSpec((tm, D_MODEL), lambda i, j: (i, 0)),
                pl.BlockSpec((1, D_MODEL), lambda i, j: (0, 0)),
                pl.BlockSpec((D_MODEL, tn), lambda i, j: (0, j))]
    args = [x, g.reshape(1, D_MODEL).astype(F32), w]
    if colvec is not None:
        in_specs.append(pl.BlockSpec((1, tn), lambda i, j: (0, j)))
        args.append(colvec.reshape(1, N).astype(F32))
    rope_k, tw = 0, LANES
    if rope is not None:
        rope_k, cos, sin = rope
        tw = cos.shape[1]
        assert tn % tw == 0
        in_specs += [pl.BlockSpec((tm, tw), lambda i, j: (i % nsb, 0))] * 2
        args += [cos, sin]
    body = functools.partial(_proj_body, headnorm=headnorm, has_cv=colvec is not None,
                             rope_k=rope_k, tw=tw)
    return pl.pallas_call(
        body, grid=(T // tm, N // tn), in_specs=in_specs,
        out_specs=pl.BlockSpec((tm, tn), lambda i, j: (i, j)),
        out_shape=jax.ShapeDtypeStruct((T, N), out_dtype),
        scratch_shapes=[pltpu.VMEM((tm, D_MODEL), BF16)],
        compiler_params=_cparams(2), name=name)(*args)


def _group_rms(o, gain_ref, width):
    parts = []
    for c in range(o.shape[1] // width):
        a = o[:, c * width:(c + 1) * width]
        ms = jnp.mean(a * a, axis=-1, keepdims=True)
        parts.append(a * lax.rsqrt(ms + EPS) * gain_ref[...])
    return jnp.concatenate(parts, axis=1)


def _outproj_body(*refs, kind):
    it = iter(refs)
    h_ref = next(it)
    if kind == "plain":
        a = next(it)[...]
    else:
        of_ref, ob_ref, g_ref, gain_ref = next(it), next(it), next(it), next(it)
        o = of_ref[...] + ob_ref[...]
        g = g_ref[...]
        if kind == "hg":
            a = (_group_rms(o, gain_ref, HG_DIM) * _sigmoid(g)).astype(BF16)
        else:
            a = ((g * _sigmoid(g)) * _group_rms(o, gain_ref, RET_DV)).astype(BF16)
    w_ref, nf_ref, wr_ref = next(it), next(it), next(it)
    hn_ref, u_ref, aff_ref = next(it), next(it), next(it)
    hn = h_ref[...] + jnp.dot(a, w_ref[...], preferred_element_type=F32)
    hn_ref[...] = hn
    ms = jnp.mean(hn * hn, axis=-1, keepdims=True)
    u = hn * lax.rsqrt(ms + EPS) * nf_ref[...]
    tm = u.shape[0]
    for s in range(SUBLANES):
        u_ref[pl.ds(s, tm, stride=SUBLANES), :] = u[:, s * LANES:(s + 1) * LANES]
    logits = lax.dot_general(wr_ref[...], u, NT, precision=lax.Precision.HIGHEST,
                             preferred_element_type=F32)
    m = jnp.max(logits, axis=0, keepdims=True)
    e = jnp.exp(logits - m)
    aff_ref[...] = e / jnp.sum(e, axis=0, keepdims=True)


def _outproj(h, mix_args, w_out, norm_g, w_router_t, kind, name):
    T = h.shape[0]
    tm = 256 if kind == "rt" else 512
    K = w_out.shape[0]
    row = lambda i: (i, 0)
    in_specs = [pl.BlockSpec((tm, D_MODEL), row)]
    args = [h]
    if kind == "plain":
        (a,) = mix_args
        in_specs.append(pl.BlockSpec((tm, K), row))
        args.append(a)
    else:
        o_f, o_b, g_arr, g_col, gain = mix_args
        gw = gain.shape[0]
        in_specs += [pl.BlockSpec((tm, K), row), pl.BlockSpec((tm, K), row),
                     pl.BlockSpec((tm, K), lambda i: (i, g_col)),
                     pl.BlockSpec((1, gw), lambda i: (0, 0))]
        args += [o_f, o_b, g_arr, gain.reshape(1, gw).astype(F32)]
    in_specs += [pl.BlockSpec((K, D_MODEL), lambda i: (0, 0)),
                 pl.BlockSpec((1, D_MODEL), lambda i: (0, 0)),
                 pl.BlockSpec((N_EXPERTS, D_MODEL), lambda i: (0, 0))]
    args += [w_out, norm_g.reshape(1, D_MODEL).astype(F32), w_router_t]
    return pl.pallas_call(
        functools.partial(_outproj_body, kind=kind), grid=(T // tm,), in_specs=in_specs,
        out_specs=[pl.BlockSpec((tm, D_MODEL), row), pl.BlockSpec((tm * SUBLANES, LANES), row),
                   pl.BlockSpec((N_EXPERTS, tm), lambda i: (0, i))],
        out_shape=[jax.ShapeDtypeStruct((T, D_MODEL), F32), jax.ShapeDtypeStruct((T * SUBLANES, LANES), F32),
                   jax.ShapeDtypeStruct((N_EXPERTS, T), F32)],
        compiler_params=_cparams(1), name=name)(*args)


NA_QROWS = 8


def _na_body(q_ref, kp_ref, kc_ref, kn_ref, vp_ref, vc_ref, vn_ref, bt_ref, o_ref, kbuf, vbuf, *, rows):
    rb = pl.program_id(2)
    blk = NA_QROWS * GRID_W
    for n, (kr, vr) in enumerate(((kp_ref, vp_ref), (kc_ref, vc_ref), (kn_ref, vn_ref))):
        kbuf[n * blk:(n + 1) * blk, :] = kr[...]
        vbuf[n * blk:(n + 1) * blk, :] = vr[...]
    lane = lax.broadcasted_iota(I32, (1, LANES), 1)
    lo = lane < 64
    zero = jnp.zeros((), BF16)

    def scores(qi):
        r = rb * NA_QROWS + qi
        r_start = jnp.clip(r - NA_WIN_ROWS // 2, 0, rows - NA_WIN_ROWS)
        roff0 = r_start - r + (NA_WIN_ROWS - 1)
        off = pl.multiple_of((r_start - rb * NA_QROWS + NA_QROWS) * GRID_W, GRID_W)
        kw = kbuf[pl.ds(off, NA_WIN_ROWS * GRID_W), :]
        qp = q_ref[qi * GRID_W:(qi + 1) * GRID_W, :]
        q2 = jnp.concatenate([jnp.where(lo, qp, zero), jnp.where(lo, zero, qp)], axis=0)
        return lax.dot_general(q2, kw, NT, preferred_element_type=F32) + bt_ref[0, roff0], off

    nxt = scores(0)
    for qi in range(NA_QROWS):
        s, off = nxt
        if qi + 1 < NA_QROWS:
            nxt = scores(qi + 1)
        p = jnp.exp2(s - jnp.max(s, axis=-1, keepdims=True))
        l = jnp.sum(p, axis=-1, keepdims=True)
        o = jnp.dot(p.astype(BF16), vbuf[pl.ds(off, NA_WIN_ROWS * GRID_W), :], preferred_element_type=F32) / l
        o_ref[qi * GRID_W:(qi + 1) * GRID_W, :] = jnp.where(lo, o[:GRID_W], o[GRID_W:]).astype(o_ref.dtype)


def _na_bias_table(rel_bias):
    c = jnp.arange(GRID_W)
    c_start = jnp.clip(c - NA_WIN_COLS // 2, 0, GRID_W - NA_WIN_COLS)
    kc = jnp.arange(GRID_W)
    valid = (kc[None, :] >= c_start[:, None]) & (kc[None, :] < c_start[:, None] + NA_WIN_COLS)
    coff = jnp.clip(kc[None, :] - c[:, None] + (NA_WIN_COLS - 1), 0, 2 * NA_WIN_COLS - 2)
    bm = jnp.where(valid[None, None], rel_bias[:, :, coff] * LOG2E, NEG)
    bt = jnp.stack([jnp.concatenate([bm[:, r0 + i] for i in range(NA_WIN_ROWS)], axis=-1)
                    for r0 in range(NA_WIN_ROWS)], axis=1).astype(F32)
    bt = bt.reshape(NA_HEADS // 2, 2, NA_WIN_ROWS, GRID_W, NA_WIN_ROWS * GRID_W)
    return jnp.swapaxes(bt, 1, 2).reshape(NA_HEADS // 2, NA_WIN_ROWS, 2 * GRID_W, NA_WIN_ROWS * GRID_W)


def _na_attention(qk, v, bt, bsz, seq_len):
    rows = seq_len // GRID_W
    assert rows % NA_QROWS == 0 and rows >= NA_WIN_ROWS
    nrb = rows // NA_QROWS
    blk = NA_QROWS * GRID_W
    npair = NA_HEADS // 2

    def spec(col0, shift):
        return pl.BlockSpec((blk, LANES),
                            lambda hp, b, rb: (b * nrb + jnp.clip(rb + shift, 0, nrb - 1), col0 + hp))
    in_specs = [spec(0, 0), spec(npair, -1), spec(npair, 0), spec(npair, 1),
                spec(0, -1), spec(0, 0), spec(0, 1),
                pl.BlockSpec((1, NA_WIN_ROWS, 2 * GRID_W, NA_WIN_ROWS * GRID_W), lambda hp, b, rb: (hp, 0, 0, 0))]
    return pl.pallas_call(
        functools.partial(_na_body, rows=rows), grid=(npair, bsz, nrb), in_specs=in_specs,
        out_specs=spec(0, 0),
        out_shape=jax.ShapeDtypeStruct((bsz * seq_len, D_MODEL), BF16),
        scratch_shapes=[pltpu.VMEM((3 * blk, LANES), BF16), pltpu.VMEM((3 * blk, LANES), BF16)],
        compiler_params=_cparams(3), name="na_attention")(qk, qk, qk, qk, v, v, v, bt)


HG_CHUNK = 128
HG_BLOCK = 512


def _bcast_row(x, group, r):
    C = x.shape[0]
    x3 = x.reshape(C // group, group, x.shape[1])
    return jnp.broadcast_to(x3[:, r:r + 1, :], x3.shape).reshape(x.shape)


def _hgrn_body(xq_ref, xi_ref, xf_ref, lb_ref, o_ref, st_ref, *, rev, nchunk):
    C = HG_CHUNK

    @pl.when(pl.program_id(2) == 0)
    def _():
        st_ref[...] = jnp.zeros_like(st_ref)

    lbv = lb_ref[...]
    row = lax.broadcasted_iota(I32, (C, C), 0)
    col = lax.broadcasted_iota(I32, (C, C), 1)
    tri = jnp.where((row <= col) if rev else (row >= col), 1.0, 0.0).astype(BF16)
    t_idx = lax.broadcasted_iota(I32, (C, HG_DIM), 0)
    sub = t_idx % SUBLANES

    def chunk(ci, carry):
        cc = (nchunk - 1 - ci) if rev else ci
        rws = pl.ds(pl.multiple_of(cc * C, C), C)
        q = xq_ref[rws, :] * (HG_DIM ** -0.5)
        xi = xi_ref[rws, :]
        v = xi * _sigmoid(xi)
        f = lbv + (1.0 - lbv) * _sigmoid(xf_ref[rws, :])
        k = 1.0 - f
        lf = jnp.log(f)
        lf_hi = lf.astype(BF16)
        lf_lo = (lf - lf_hi.astype(F32)).astype(BF16)
        b = (jnp.dot(tri, lf_hi, preferred_element_type=F32)
             + jnp.dot(tri, lf_lo, preferred_element_type=F32))
        vb = v.astype(BF16)
        state = st_ref[...]
        o = jnp.dot((q * jnp.exp(b)).astype(BF16), state.astype(BF16), preferred_element_type=F32)
        for s in range(SUBLANES):
            msk = (sub <= s) if rev else (sub >= s)
            e = jnp.exp(jnp.where(msk, b - _bcast_row(b, SUBLANES, s), NEG))
            sc = jnp.sum(q * _bcast_row(k, SUBLANES, s) * e, axis=-1, keepdims=True)
            o = o + sc * _bcast_row(v, SUBLANES, s)
        scores = jnp.zeros((C, C), F32)
        m = SUBLANES
        while m < C:
            right = ((t_idx // m) % 2) == 1
            rr = _bcast_row(b, 2 * m, m if rev else m - 1)
            qside = jnp.logical_not(right) if rev else right
            kside = right if rev else jnp.logical_not(right)
            qe = jnp.where(qside, q * jnp.exp(jnp.where(qside, b - rr, 0.0)), 0.0)
            ke = jnp.where(kside, k * jnp.exp(jnp.where(kside, rr - b, 0.0)), 0.0)
            sl = lax.dot_general(qe.astype(BF16), ke.astype(BF16), NT, preferred_element_type=F32)
            scores = scores + jnp.where((row // (2 * m)) == (col // (2 * m)), sl, 0.0)
            m *= 2
        o = o + jnp.dot(scores.astype(BF16), vb, preferred_element_type=F32)
        o_ref[rws, :] = o
        bl = b[0:1, :] if rev else b[C - 1:C, :]
        ke = (k * jnp.exp(bl - b)).astype(BF16)
        upd = lax.dot_general(ke, vb, TN, preferred_element_type=F32)
        decay = jnp.transpose(jnp.broadcast_to(jnp.exp(bl), (HG_DIM, HG_DIM)))
        st_ref[...] = state * decay + upd
        return carry

    lax.fori_loop(0, nchunk, chunk, 0)


def _hgrn_direction(y, lb, bsz, seq_len, rev):
    lbk = min(HG_BLOCK, seq_len)
    assert seq_len % lbk == 0 and lbk % HG_CHUNK == 0
    nb = seq_len // lbk
    fpart = 3 if rev else 2

    def spec(part):
        return pl.BlockSpec((lbk, HG_DIM),
                            lambda b, h, c: (b * nb + ((nb - 1 - c) if rev else c), part * HG_HEADS + h))
    return pl.pallas_call(
        functools.partial(_hgrn_body, rev=rev, nchunk=lbk // HG_CHUNK),
        grid=(bsz, HG_HEADS, nb),
        in_specs=[spec(0), spec(1), spec(fpart), pl.BlockSpec((1, HG_DIM), lambda b, h, c: (0, h))],
        out_specs=spec(0),
        out_shape=jax.ShapeDtypeStruct((bsz * seq_len, D_MODEL), F32),
        scratch_shapes=[pltpu.VMEM((HG_DIM, HG_DIM), F32)],
        compiler_params=_cparams(3), name="hgrn_bwd" if rev else "hgrn_fwd")(y, y, y, lb.reshape(1, D_MODEL))


def _flash_body(q_ref, k_ref, v_ref, o_ref, m_sc, acc_sc):
    ki = pl.program_id(3)

    @pl.when(ki == 0)
    def _():
        m_sc[...] = jnp.full_like(m_sc, NEG)
        acc_sc[...] = jnp.zeros_like(acc_sc)

    lane = lax.broadcasted_iota(I32, (1, LANES), 1)
    lo = lane < 64
    zero = jnp.zeros((), BF16)
    k = k_ref[...]
    v1 = jnp.where(lo, v_ref[...], jnp.ones((), BF16))

    def scores(hd):
        qp = q_ref[:, (hd // 2) * LANES:(hd // 2 + 1) * LANES]
        qm = jnp.where(lo if hd % 2 == 0 else jnp.logical_not(lo), qp, zero)
        return lax.dot_general(qm, k, NT, preferred_element_type=F32)

    n_heads = GQA_HEADS // GQA_KV
    s_next = scores(0)
    for hd in range(n_heads):
        s = s_next
        if hd + 1 < n_heads:
            s_next = scores(hd + 1)
        m_prev = m_sc[hd]
        m_new = jnp.maximum(m_prev, jnp.max(s, axis=-1, keepdims=True))
        p = jnp.exp2(s - m_new)
        acc_sc[hd] = (jnp.exp2(m_prev - m_new) * acc_sc[hd]
                      + jnp.dot(p.astype(BF16), v1, preferred_element_type=F32))
        m_sc[hd] = m_new

    @pl.when(ki == pl.num_programs(3) - 1)
    def _():
        for pair in range(2):
            a0 = acc_sc[2 * pair]
            a1 = pltpu.roll(acc_sc[2 * pair + 1], 64, 1)
            o_ref[:, pair * LANES:(pair + 1) * LANES] = jnp.where(
                lo, a0 / a0[:, 64:65], a1 / a1[:, 0:1]).astype(o_ref.dtype)


def _flash_gqa(qk, v, bsz, seq_len):
    tq = min(256, seq_len)
    tk = min(8192, seq_len)
    nq, nk = seq_len // tq, seq_len // tk
    gw = (GQA_HEADS // GQA_KV) * GQA_DH
    return pl.pallas_call(
        _flash_body, grid=(bsz, GQA_KV, nq, nk),
        in_specs=[pl.BlockSpec((tq, gw), lambda b, g, i, j: (b * nq + i, g)),
                  pl.BlockSpec((tk, LANES), lambda b, g, i, j: (b * nk + j, D_MODEL // LANES + g)),
                  pl.BlockSpec((tk, LANES), lambda b, g, i, j: (b * nk + j, g))],
        out_specs=pl.BlockSpec((tq, gw), lambda b, g, i, j: (b * nq + i, g)),
        out_shape=jax.ShapeDtypeStruct((bsz * seq_len, D_MODEL), BF16),
        scratch_shapes=[pltpu.VMEM((4, tq, 1), F32), pltpu.VMEM((4, tq, LANES), F32)],
        compiler_params=_cparams(4), name="flash_gqa")(qk, qk, v)


RET_CHUNK = 128
RET_BLOCK = 512


def _ret_body(q_ref, k_ref, v_ref, dm_ref, qd_ref, kd_ref, cd_ref, o_ref, st_ref, *, rev, nchunk):
    C = RET_CHUNK

    @pl.when(pl.program_id(2) == 0)
    def _():
        st_ref[...] = jnp.zeros_like(st_ref)

    def chunk(ci, carry):
        cc = (nchunk - 1 - ci) if rev else ci
        rws = pl.ds(pl.multiple_of(cc * C, C), C)
        q = q_ref[rws, :]
        k = k_ref[rws, :]
        v = v_ref[rws, :]
        state = st_ref[...]
        s = lax.dot_general(q, k, NT, preferred_element_type=F32) * dm_ref[0]
        o = jnp.dot(s.astype(BF16), v, preferred_element_type=F32)
        qs = (q.astype(F32) * qd_ref[0]).astype(BF16)
        o = o + jnp.dot(qs, state.astype(BF16), preferred_element_type=F32)
        o_ref[rws, :] = o
        ks = (k.astype(F32) * kd_ref[0]).astype(BF16)
        st_ref[...] = state * cd_ref[0] + lax.dot_general(ks, v, TN, preferred_element_type=F32)
        return carry

    lax.fori_loop(0, nchunk, chunk, 0)


def _ret_tables(rev):
    C = RET_CHUNK
    log_gamma = jnp.log1p(-jnp.exp2(-5.0 - jnp.arange(RET_HEADS, dtype=F32)))[:, None, None]
    pos = jnp.arange(C, dtype=F32)
    rel = pos[:, None] - pos[None, :]
    if rev:
        rel = -rel
        qpow, kpow = C - pos, pos
    else:
        qpow, kpow = pos + 1.0, C - 1.0 - pos
    dm = jnp.where(rel[None] >= 0, jnp.exp(rel[None] * log_gamma), 0.0)
    qd = jnp.broadcast_to(jnp.exp(qpow[None, :, None] * log_gamma), (RET_HEADS, C, RET_DK))
    kd = jnp.broadcast_to(jnp.exp(kpow[None, :, None] * log_gamma), (RET_HEADS, C, RET_DK))
    cd = jnp.broadcast_to(jnp.exp(C * log_gamma), (RET_HEADS, 1, RET_DV))
    return dm.astype(F32), qd.astype(F32), kd.astype(F32), cd.astype(F32)


def _ret_direction(qk, v, bsz, seq_len, rev):
    lbk = min(RET_BLOCK, seq_len)
    nb = seq_len // lbk
    rowblk = lambda b, h, c: b * nb + ((nb - 1 - c) if rev else c)
    tab = lambda shape: pl.BlockSpec((1,) + shape, lambda b, h, c: (h, 0, 0))
    return pl.pallas_call(
        functools.partial(_ret_body, rev=rev, nchunk=lbk // RET_CHUNK),
        grid=(bsz, RET_HEADS, nb),
        in_specs=[pl.BlockSpec((lbk, RET_DK), lambda b, h, c: (rowblk(b, h, c), h)),
                  pl.BlockSpec((lbk, RET_DK), lambda b, h, c: (rowblk(b, h, c), RET_HEADS + h)),
                  pl.BlockSpec((lbk, RET_DV), lambda b, h, c: (rowblk(b, h, c), h)),
                  tab((RET_CHUNK, RET_CHUNK)), tab((RET_CHUNK, RET_DK)), tab((RET_CHUNK, RET_DK)),
                  tab((1, RET_DV))],
        out_specs=pl.BlockSpec((lbk, RET_DV), lambda b, h, c: (rowblk(b, h, c), h)),
        out_shape=jax.ShapeDtypeStruct((bsz * seq_len, RET_HEADS * RET_DV), F32),
        scratch_shapes=[pltpu.VMEM((RET_DK, RET_DV), F32)],
        compiler_params=_cparams(3), name="ret_bwd" if rev else "ret_fwd")(qk, qk, v, *_ret_tables(rev))


MOE_FF_CHUNK = 512


def _moe_body(idx_ref, nxt_ref, u_hbm, gate_ref, wg_ref, wu_ref, wd_ref, hi_ref, lo_ref, xbuf, sem, *, tc):
    nsteps = pl.num_programs(0) * pl.num_programs(1)
    step = pl.program_id(0) * pl.num_programs(1) + pl.program_id(1)
    slot = step % 2

    def row_copy(token, r, s):
        src = u_hbm.at[pl.ds(pl.multiple_of(token * SUBLANES, SUBLANES), SUBLANES)]
        return pltpu.make_async_copy(src, xbuf.at[s, pl.ds(r * SUBLANES, SUBLANES)], sem.at[s])

    def wait_rows(s):
        for _ in range(tc):
            row_copy(0, 0, s).wait()

    @pl.when(step == 0)
    def _():
        def issue(r, carry):
            row_copy(idx_ref[0, 0, r], r, 0).start()
            return carry
        lax.fori_loop(0, tc, issue, 0)

    wait_rows(slot)
    x = jnp.concatenate([xbuf[slot, pl.ds(s, tc, stride=SUBLANES), :] for s in range(SUBLANES)],
                        axis=1).astype(BF16)
    for r in range(tc):
        row_copy(nxt_ref[0, 0, r], r, 1 - slot).start()
    acc = jnp.zeros((tc, D_MODEL), F32)
    for f in range(EXPERT_FF // MOE_FF_CHUNK):
        cols = slice(f * MOE_FF_CHUNK, (f + 1) * MOE_FF_CHUNK)
        g = jnp.dot(x, wg_ref[0, :, cols], preferred_element_type=F32)
        up = jnp.dot(x, wu_ref[0, :, cols], preferred_element_type=F32)
        hid = ((g * _sigmoid(g)) * up).astype(BF16)
        acc = acc + jnp.dot(hid, wd_ref[0, cols, :], preferred_element_type=F32)
    gate = gate_ref[0]
    for c in range(D_MODEL // LANES):
        cols = slice(c * LANES, (c + 1) * LANES)
        ye = acc[:, cols] * gate
        hi = ye.astype(BF16)
        hi_ref[0, :, cols] = hi
        lo_ref[0, :, cols] = (ye - hi.astype(F32)).astype(BF16)

    @pl.when(step == nsteps - 1)
    def _():
        wait_rows(1 - slot)


def _moe_experts(u, idx_sorted, gate_sorted, wg, wu, wd):
    cap = idx_sorted.shape[1]
    tc = min(512, cap)
    nj = cap // tc
    nsteps = N_EXPERTS * nj
    idx3 = idx_sorted.reshape(nsteps, 1, tc)
    gate_b = jnp.broadcast_to(gate_sorted[:, :, None], (N_EXPERTS, cap, LANES))
    wspec = lambda shape: pl.BlockSpec((1,) + shape, lambda e, j: (e, 0, 0))
    ospec = pl.BlockSpec((1, tc, D_MODEL), lambda e, j: (e, j, 0))
    oshape = jax.ShapeDtypeStruct((N_EXPERTS, cap, D_MODEL), BF16)
    return pl.pallas_call(
        functools.partial(_moe_body, tc=tc), grid=(N_EXPERTS, nj),
        in_specs=[pl.BlockSpec((1, 1, tc), lambda e, j: (e * nj + j, 0, 0), memory_space=pltpu.SMEM),
                  pl.BlockSpec((1, 1, tc), lambda e, j: (jnp.minimum(e * nj + j + 1, nsteps - 1), 0, 0),
                               memory_space=pltpu.SMEM),
                  pl.BlockSpec(memory_space=pl.ANY),
                  pl.BlockSpec((1, tc, LANES), lambda e, j: (e, j, 0)),
                  wspec((D_MODEL, EXPERT_FF)), wspec((D_MODEL, EXPERT_FF)), wspec((EXPERT_FF, D_MODEL))],
        out_specs=[ospec, ospec], out_shape=[oshape, oshape],
        scratch_shapes=[pltpu.VMEM((2, tc * SUBLANES, LANES), F32), pltpu.SemaphoreType.DMA((2,))],
        compiler_params=_cparams(2), name="moe_experts")(idx3, idx3, u, gate_b, wg, wu, wd)


CMB_TOKENS = 256
CMB_WINDOW = LANES // 2
CMB_ALIGN = 2 * SUBLANES


def _combine_body(base_ref, nr_ref, h_ref, pos_ref, hi_hbm, lo_hbm, o_ref, buf, xbuf, sem, xsem, *, cap, ntiles):
    i = pl.program_id(0)
    slot = i % 2
    Wn = CMB_WINDOW
    lane = lax.broadcasted_iota(I32, (1, LANES), 1)
    lo_half = lane < Wn

    def window(tile, e, k):
        start = (base_ref[e * ntiles + tile] // CMB_ALIGN) * CMB_ALIGN + k * Wn
        w0 = pl.multiple_of(jnp.minimum(start, cap - Wn), CMB_ALIGN)
        return start, w0

    def copies(tile, k, dst, dsem):
        out = []
        for e in range(N_EXPERTS):
            w0 = window(tile, e, k)[1]
            out.append(pltpu.make_async_copy(hi_hbm.at[e, pl.ds(w0, Wn)], dst.at[0, e], dsem))
            out.append(pltpu.make_async_copy(lo_hbm.at[e, pl.ds(w0, Wn)], dst.at[1, e], dsem))
        return out

    def one_hot(k, check_start):
        parts = []
        for e in range(0, N_EXPERTS, 2):
            s0, w0 = window(i, e, k)
            s1, w1 = window(i, e + 1, k)
            p0 = pos_ref[:, e:e + 1]
            p1 = pos_ref[:, e + 1:e + 2]
            hit = jnp.where(lo_half, p0 - w0, p1 - w1 + Wn) == lane
            if check_start:
                hit = jnp.logical_and(hit, jnp.where(lo_half, p0 - s0, p1 - s1) >= 0)
            parts.append(jnp.where(hit, 1.0, 0.0).astype(BF16))
        return jnp.concatenate(parts, axis=1)

    def placed(smat, src):
        return (jnp.dot(smat, src[0].reshape(N_EXPERTS * Wn, D_MODEL), preferred_element_type=F32)
                + jnp.dot(smat, src[1].reshape(N_EXPERTS * Wn, D_MODEL), preferred_element_type=F32))

    @pl.when(i == 0)
    def _():
        for c in copies(0, 0, buf.at[0], sem.at[0]):
            c.start()

    @pl.when(i + 1 < ntiles)
    def _():
        for c in copies(i + 1, 0, buf.at[1 - slot], sem.at[1 - slot]):
            c.start()

    smat = one_hot(0, False)
    for c in copies(i, 0, buf.at[slot], sem.at[slot]):
        c.wait()
    acc = h_ref[...] + placed(smat, buf[slot])

    def extra_round(k, acc):
        for c in copies(i, k, xbuf, xsem):
            c.start()
        smat = one_hot(k, True)
        for c in copies(i, k, xbuf, xsem):
            c.wait()
        return acc + placed(smat, xbuf[...])

    o_ref[...] = lax.fori_loop(1, nr_ref[i], extra_round, acc)


def _moe_combine(h, ye_hi, ye_lo, pos_t, base, nrounds):
    T = h.shape[0]
    cap = ye_hi.shape[1]
    ntiles = T // CMB_TOKENS
    assert cap >= CMB_WINDOW and cap % CMB_ALIGN == 0
    wshape = (2, N_EXPERTS, CMB_WINDOW, D_MODEL)
    grid_spec = pltpu.PrefetchScalarGridSpec(
        num_scalar_prefetch=2, grid=(ntiles,),
        in_specs=[pl.BlockSpec((CMB_TOKENS, D_MODEL), lambda i, b, n: (i, 0)),
                  pl.BlockSpec((CMB_TOKENS, N_EXPERTS), lambda i, b, n: (i, 0)),
                  pl.BlockSpec(memory_space=pl.ANY), pl.BlockSpec(memory_space=pl.ANY)],
        out_specs=pl.BlockSpec((CMB_TOKENS, D_MODEL), lambda i, b, n: (i, 0)),
        scratch_shapes=[pltpu.VMEM((2,) + wshape, BF16), pltpu.VMEM(wshape, BF16),
                        pltpu.SemaphoreType.DMA((2,)), pltpu.SemaphoreType.DMA(())])
    return pl.pallas_call(
        functools.partial(_combine_body, cap=cap, ntiles=ntiles), grid_spec=grid_spec,
        out_shape=jax.ShapeDtypeStruct((T, D_MODEL), F32),
        compiler_params=_cparams(1), name="moe_combine")(base.reshape(-1), nrounds, h, pos_t, ye_hi, ye_lo)


def _flag(cond):
    return jnp.where(cond, 1.0, 0.0)


def _select_body(aff_ref, pos_ref, idx_ref, gate_ref, thr_sc, *, cap):
    n_exp, n_chunk, _ = aff_ref.shape

    def bit_step(i, prefix):
        cand = prefix | jnp.left_shift(jnp.int32(1), 30 - i)
        hit = _flag(pltpu.bitcast(aff_ref[...], I32) >= cand)
        cnt = jnp.sum(jnp.sum(hit, axis=1, keepdims=True), axis=2, keepdims=True)
        return jnp.where(cnt >= cap, cand, prefix)

    thr = lax.fori_loop(0, 31, bit_step, jnp.zeros((n_exp, 1, 1), I32))
    thr_sc[...] = jnp.broadcast_to(thr, thr_sc.shape)

    li = lax.broadcasted_iota(I32, (LANES, LANES), 0)
    lj = lax.broadcasted_iota(I32, (LANES, LANES), 1)
    upper = _flag(li <= lj).astype(BF16)
    ci = lax.broadcasted_iota(I32, (n_chunk, n_chunk), 0)
    cj = lax.broadcasted_iota(I32, (n_chunk, n_chunk), 1)
    before = _flag(cj < ci).astype(BF16)
    chunk_id = lax.broadcasted_iota(I32, (n_chunk, LANES), 0).astype(F32)
    lane_id = lax.broadcasted_iota(I32, (LANES, LANES), 0).astype(F32)
    slot_lane = lax.broadcasted_iota(I32, (1, LANES), 1)

    def counts(flags):
        local = jnp.dot(flags.astype(BF16), upper, preferred_element_type=F32)
        total = jnp.broadcast_to(local[:, LANES - 1:LANES], local.shape)
        return local, total, jnp.dot(before, total.astype(BF16), preferred_element_type=F32)

    def per_expert(e, carry):
        aff = aff_ref[e]
        key = pltpu.bitcast(aff, I32)
        t = thr_sc[e]
        gt = _flag(key > t)
        eq = _flag(key == t)
        need = cap - jnp.sum(jnp.sum(gt, axis=1, keepdims=True), axis=0, keepdims=True)
        eq_local, _, eq_off = counts(eq)
        sel = gt + eq * _flag(eq_local + eq_off - eq < need)
        local, total, off = counts(sel)
        pos_ref[e] = jnp.where(sel > 0.0, local + off - 1.0, -1.0).astype(I32)
        reached = off + total
        local_t = jnp.transpose(local).astype(BF16)
        aff_tr = jnp.transpose(aff)

        def per_row(r, c2):
            slot = (r * LANES + slot_lane).astype(F32)
            chunk = jnp.sum(_flag(reached <= slot), axis=0, keepdims=True)
            skipped = jnp.sum(jnp.where(chunk_id < chunk, total, 0.0), axis=0, keepdims=True)
            pick = _flag(chunk_id == chunk)
            run = jnp.dot(local_t, pick.astype(BF16), preferred_element_type=F32)
            lane = jnp.sum(_flag(run <= slot - skipped), axis=0, keepdims=True)
            idx_ref[e, pl.ds(r, 1), :] = (chunk * LANES + lane).astype(I32)
            vals = jnp.dot(aff_tr, pick, precision=lax.Precision.HIGHEST, preferred_element_type=F32)
            gate_ref[e, pl.ds(r, 1), :] = jnp.sum(jnp.where(lane_id == lane, vals, 0.0), axis=0, keepdims=True)
            return c2

        lax.fori_loop(0, cap // LANES, per_row, 0)
        return carry

    lax.fori_loop(0, n_exp, per_expert, 0)


def _select(aff_t, cap):
    E, T = aff_t.shape
    assert T % LANES == 0 and cap % LANES == 0
    full = lambda shape: pl.BlockSpec(shape, lambda i: (0, 0, 0))
    shapes = [(E, T // LANES, LANES), (E, cap // LANES, LANES), (E, cap // LANES, LANES)]
    pos, idx, gate = pl.pallas_call(
        functools.partial(_select_body, cap=cap), grid=(1,),
        in_specs=[full(shapes[0])], out_specs=[full(s) for s in shapes],
        out_shape=[jax.ShapeDtypeStruct(shapes[0], I32), jax.ShapeDtypeStruct(shapes[1], I32),
                   jax.ShapeDtypeStruct(shapes[2], F32)],
        scratch_shapes=[pltpu.VMEM((E, 1, LANES), I32)],
        compiler_params=_cparams(1), name="moe_select")(aff_t.reshape(shapes[0]))
    return pos.reshape(E, T), idx.reshape(E, cap), gate.reshape(E, cap)


def _route(aff_t, cap):
    E, T = aff_t.shape
    ntiles = T // CMB_TOKENS
    pos, idx_s, gate_s = _select(aff_t, cap)
    cnt = jnp.sum((pos >= 0).reshape(E, ntiles, CMB_TOKENS), axis=-1, dtype=I32)
    base = jnp.cumsum(cnt, axis=1, dtype=I32) - cnt
    need = jnp.where(cnt > 0, (base % CMB_ALIGN + cnt + CMB_WINDOW - 1) // CMB_WINDOW, 0)
    return idx_s, gate_s, pos.T, base, jnp.max(need, axis=0).astype(I32)


def _moe_layer(h, u, aff_t, wg, wu, wd):
    T = h.shape[0]
    cap = CAPACITY_FACTOR * T // N_EXPERTS
    idx_s, gate_s, pos_t, base, nrounds = _route(aff_t, cap)
    ye_hi, ye_lo = _moe_experts(u, idx_s, gate_s, wg, wu, wd)
    return _moe_combine(h, ye_hi, ye_lo, pos_t, base, nrounds)


def _rope_tables(seq_len, n_freq, reps):
    t = jnp.arange(seq_len)
    inv_freq = ROPE_THETA ** (-jnp.arange(n_freq, dtype=F32) / n_freq)
    out = []
    for pos in ((t // GRID_W).astype(F32), (t % GRID_W).astype(F32)):
        ang = pos[:, None] * inv_freq[None, :]
        c = jnp.concatenate([jnp.cos(ang), jnp.cos(ang)], axis=1)
        s = jnp.concatenate([-jnp.sin(ang), jnp.sin(ang)], axis=1)
        out.append((c, s))
    cos = jnp.concatenate([out[0][0], out[1][0]], axis=1)
    sin = jnp.concatenate([out[0][1], out[1][1]], axis=1)
    return jnp.tile(cos, (1, reps)), jnp.tile(sin, (1, reps))


def _dup_heads(w, n_heads, dh):
    k = w.shape[0]
    return jnp.repeat(w.reshape(k, n_heads, 1, dh), 2, axis=2).reshape(k, n_heads * 2 * dh)


def _trunk(x, p):
    bsz, seq_len, _ = x.shape
    T = bsz * seq_len
    h = x.reshape(T, D_MODEL)
    for layer in range(4):
        nm = p["norm_mix"][layer]
        if layer == 0:
            qk = _proj(h, nm, p["na_wqk"], seq_len, BF16, headnorm=True, colvec=p["na_qk_gain"], name="na_proj_qk")
            v = _proj(h, nm, p["na_wv"], seq_len, BF16, name="na_proj_v")
            mix_args = (_na_attention(qk, v, p["na_bt"], bsz, seq_len),)
            kind, w_out = "plain", p["na_wo"]
        elif layer == 1:
            y = _proj(h, nm, p["hg_win"], seq_len, F32, name="hg_proj")
            o_f = _hgrn_direction(y, p["hg_lb"], bsz, seq_len, False)
            o_b = _hgrn_direction(y, p["hg_lb"], bsz, seq_len, True)
            mix_args = (o_f, o_b, y, 4, p["hg_o_gain"])
            kind, w_out = "hg", p["hg_wo"]
        elif layer == 2:
            cos, sin = _rope_tables(seq_len, 16, 2)
            qk = _proj(h, nm, p["gq_wqk"], seq_len, BF16, headnorm=True, colvec=p["gq_qk_gain"],
                       rope=(16, cos, sin), name="gqa_proj_qk")
            v = _proj(h, nm, p["gq_wv"], seq_len, BF16, name="gqa_proj_v")
            mix_args = (_flash_gqa(qk, v, bsz, seq_len),)
            kind, w_out = "plain", p["gq_wo"]
        else:
            cos, sin = _rope_tables(seq_len, 64, 1)
            qk = _proj(h, nm, p["rt_wqk"], seq_len, BF16, colvec=p["rt_qk_scale"],
                       rope=(64, cos, sin), name="ret_proj_qk")
            v = _proj(h, nm, p["rt_wv"], seq_len, BF16, name="ret_proj_v")
            g = _proj(h, nm, p["rt_wg"], seq_len, F32, name="ret_proj_g")
            o_f = _ret_direction(qk, v, bsz, seq_len, False)
            o_b = _ret_direction(qk, v, bsz, seq_len, True)
            mix_args = (o_f, o_b, g, 0, p["rt_o_gain"])
            kind, w_out = "rt", p["rt_wo"]
        h, u, aff_t = _outproj(h, mix_args, w_out, p["norm_ffn"][layer], p["router_t"][layer], kind,
                               name=f"outproj_{kind}")
        h = _moe_layer(h, u, aff_t, p["moe_wg"][layer], p["moe_wu"][layer], p["moe_wd"][layer])
    return h.reshape(bsz, seq_len, D_MODEL)


def kernel(x_prompt, x_sample, norm_mix, norm_ffn, na_w_in, na_q_gain, na_k_gain, na_rel_bias, na_w_out, hg_w_in, hg_lb, hg_o_gain, hg_w_out, gq_w_in, gq_q_gain, gq_k_gain, gq_w_out, rt_w_in, rt_o_gain, rt_w_out, moe_router, moe_w_gate, moe_w_up, moe_w_down):
    bf = lambda a: a.astype(BF16)
    na_w, gq_w, rt_w = na_w_in[0], gq_w_in[0], rt_w_in[0]
    lb_cum = jnp.cumsum(jax.nn.softmax(hg_lb.astype(F32), axis=0), axis=0)
    qd = GQA_HEADS * GQA_DH
    kd = GQA_KV * GQA_DH
    rq = RET_HEADS * RET_DK
    rv = RET_HEADS * RET_DV
    p = {
        "norm_mix": norm_mix, "norm_ffn": norm_ffn,
        "na_wqk": bf(na_w[:, :2 * D_MODEL]), "na_wv": bf(na_w[:, 2 * D_MODEL:]),
        "na_qk_gain": jnp.concatenate([jnp.tile(na_q_gain[0], NA_HEADS) * (64 ** -0.5 * LOG2E),
                                       jnp.tile(na_k_gain[0], NA_HEADS)]),
        "na_bt": _na_bias_table(na_rel_bias[0]), "na_wo": bf(na_w_out[0]),
        "hg_win": bf(hg_w_in[0]), "hg_lb": lb_cum[1] - lb_cum[0], "hg_o_gain": hg_o_gain[0],
        "hg_wo": bf(hg_w_out[0]),
        "gq_wqk": bf(jnp.concatenate([gq_w[:, :qd], _dup_heads(gq_w[:, qd:qd + kd], GQA_KV, GQA_DH)], axis=1)),
        "gq_wv": bf(_dup_heads(gq_w[:, qd + kd:], GQA_KV, GQA_DH)),
        "gq_qk_gain": jnp.concatenate([jnp.tile(gq_q_gain[0], GQA_HEADS) * (GQA_DH ** -0.5 * LOG2E),
                                       jnp.tile(gq_k_gain[0], 2 * GQA_KV)]),
        "gq_wo": bf(gq_w_out[0]),
        "rt_wqk": bf(rt_w[:, :2 * rq]), "rt_wv": bf(rt_w[:, 2 * rq:2 * rq + rv]), "rt_wg": bf(rt_w[:, 2 * rq + rv:]),
        "rt_qk_scale": jnp.concatenate([jnp.full((rq,), RET_DK ** -0.5, F32), jnp.ones((rq,), F32)]),
        "rt_o_gain": rt_o_gain[0], "rt_wo": bf(rt_w_out[0]),
        "router_t": jnp.swapaxes(moe_router, 1, 2).astype(F32),
        "moe_wg": bf(moe_w_gate), "moe_wu": bf(moe_w_up), "moe_wd": bf(moe_w_down),
    }
    return (_trunk(x_prompt, p), _trunk(x_sample, p))
```

```python
import functools
import math

import jax
import jax.numpy as jnp
from jax import lax
from jax.experimental import pallas as pl
from jax.experimental.pallas import tpu as pltpu

F32 = jnp.float32
BF16 = jnp.bfloat16
I32 = jnp.int32

D_MODEL = 1024
GRID_W = 64
EPS = 1e-6
ROPE_THETA = 10000.0
NEG = -1e30
LOG2E = math.log2(math.e)

NA_HEADS = 16
NA_WIN_ROWS = 8
NA_WIN_COLS = 16
HG_HEADS = 8
HG_DIM = 128
GQA_HEADS = 16
GQA_KV = 4
GQA_DH = 64
RET_HEADS = 4
RET_DK = 256
RET_DV = 512
N_EXPERTS = 16
EXPERT_FF = 2048
CAPACITY_FACTOR = 2

LANES = 128
SUBLANES = 8
VMEM_LIMIT = 52 * 1024 * 1024

NT = (((1,), (1,)), ((), ()))
TN = (((0,), (0,)), ((), ()))


def _cparams(n_axes):
    return pltpu.CompilerParams(dimension_semantics=("arbitrary",) * n_axes,
                                vmem_limit_bytes=VMEM_LIMIT)


def _sigmoid(x):
    return 1.0 / (1.0 + jnp.exp(-x))


def _proj_body(*refs, headnorm, has_cv, rope_k, tw):
    it = iter(refs)
    x_ref, g_ref, w_ref = next(it), next(it), next(it)
    cv_ref = next(it) if has_cv else None
    cos_ref = next(it) if rope_k else None
    sin_ref = next(it) if rope_k else None
    o_ref, xn_ref = next(it), next(it)

    @pl.when(pl.program_id(1) == 0)
    def _():
        x = x_ref[...]
        ms = jnp.mean(x * x, axis=-1, keepdims=True)
        xn_ref[...] = (x * lax.rsqrt(ms + EPS) * g_ref[...]).astype(BF16)

    acc = jnp.dot(xn_ref[...], w_ref[...], preferred_element_type=F32)
    if not (headnorm or has_cv or rope_k):
        o_ref[...] = acc.astype(o_ref.dtype)
        return
    lane = lax.broadcasted_iota(I32, (1, LANES), 1)
    lo = lane < 64
    for c in range(acc.shape[1] // LANES):
        cols = slice(c * LANES, (c + 1) * LANES)
        a = acc[:, cols]
        if headnorm:
            sq = a * a
            s_lo = jnp.sum(jnp.where(lo, sq, 0.0), axis=-1, keepdims=True)
            s_hi = jnp.sum(jnp.where(lo, 0.0, sq), axis=-1, keepdims=True)
            a = a * lax.rsqrt(jnp.where(lo, s_lo, s_hi) * (1.0 / 64.0) + EPS)
        if has_cv:
            a = a * cv_ref[:, cols]
        if rope_k:
            tcol = (c * LANES) % tw
            if rope_k == 64:
                partner = pltpu.roll(a, 64, 1)
            else:
                partner = jnp.where((lane & rope_k) != 0, pltpu.roll(a, rope_k, 1),
                                    pltpu.roll(a, LANES - rope_k, 1))
            a = a * cos_ref[:, tcol:tcol + LANES] + partner * sin_ref[:, tcol:tcol + LANES]
        o_ref[:, cols] = a.astype(o_ref.dtype)


def _proj(x, g, w, seq_len, out_dtype, *, headnorm=False, colvec=None, rope=None, name="proj"):
    T = x.shape[0]
    N = w.shape[1]
    tm = min(1024, seq_len)
    tn = 512 if N % 512 == 0 else 256
    assert T % tm == 0 and seq_len % tm == 0 and N % tn == 0
    nsb = seq_len // tm
    in_specs = [pl.BlockSpec((tm, D_MODEL), lambda i, j: (i, 0)),
                pl.BlockSpec((1, D_MODEL), lambda i, j: (0, 0)),
                pl.BlockSpec((D_MODEL, tn), lambda i, j: (0, j))]
    args = [x, g.reshape(1, D_MODEL).astype(F32), w]
    if colvec is not None:
        in_specs.append(pl.BlockSpec((1, tn), lambda i, j: (0, j)))
        args.append(colvec.reshape(1, N).astype(F32))
    rope_k, tw = 0, LANES
    if rope is not None:
        rope_k, cos, sin = rope
        tw = cos.shape[1]
        assert tn % tw == 0
        in_specs += [pl.BlockSpec((tm, tw), lambda i, j: (i % nsb, 0))] * 2
        args += [cos, sin]
    body = functools.partial(_proj_body, headnorm=headnorm, has_cv=colvec is not None,
                             rope_k=rope_k, tw=tw)
    return pl.pallas_call(
        body, grid=(T // tm, N // tn), in_specs=in_specs,
        out_specs=pl.BlockSpec((tm, tn), lambda i, j: (i, j)),
        out_shape=jax.ShapeDtypeStruct((T, N), out_dtype),
        scratch_shapes=[pltpu.VMEM((tm, D_MODEL), BF16)],
        compiler_params=_cparams(2), name=name)(*args)


def _group_rms(o, gain_ref, width):
    parts = []
    for c in range(o.shape[1] // width):
        a = o[:, c * width:(c + 1) * width]
        ms = jnp.mean(a * a, axis=-1, keepdims=True)
        parts.append(a * lax.rsqrt(ms + EPS) * gain_ref[...])
    return jnp.concatenate(parts, axis=1)


def _outproj_body(*refs, kind):
    it = iter(refs)
    h_ref = next(it)
    if kind == "plain":
        a = next(it)[...]
    else:
        of_ref, ob_ref, g_ref, gain_ref = next(it), next(it), next(it), next(it)
        o = of_ref[...] + ob_ref[...]
        g = g_ref[...]
        if kind == "hg":
            a = (_group_rms(o, gain_ref, HG_DIM) * _sigmoid(g)).astype(BF16)
        else:
            a = ((g * _sigmoid(g)) * _group_rms(o, gain_ref, RET_DV)).astype(BF16)
    w_ref, nf_ref, wr_ref = next(it), next(it), next(it)
    hn_ref, u_ref, aff_ref = next(it), next(it), next(it)
    hn = h_ref[...] + jnp.dot(a, w_ref[...], preferred_element_type=F32)
    hn_ref[...] = hn
    ms = jnp.mean(hn * hn, axis=-1, keepdims=True)
    u = hn * lax.rsqrt(ms + EPS) * nf_ref[...]
    tm = u.shape[0]
    for s in range(SUBLANES):
        u_ref[pl.ds(s, tm, stride=SUBLANES), :] = u[:, s * LANES:(s + 1) * LANES]
    logits = lax.dot_general(wr_ref[...], u, NT, precision=lax.Precision.HIGHEST,
                             preferred_element_type=F32)
    m = jnp.max(logits, axis=0, keepdims=True)
    e = jnp.exp(logits - m)
    aff_ref[...] = e / jnp.sum(e, axis=0, keepdims=True)


def _outproj(h, mix_args, w_out, norm_g, w_router_t, kind, name):
    T = h.shape[0]
    tm = 256 if kind == "rt" else 512
    K = w_out.shape[0]
    row = lambda i: (i, 0)
    in_specs = [pl.BlockSpec((tm, D_MODEL), row)]
    args = [h]
    if kind == "plain":
        (a,) = mix_args
        in_specs.append(pl.BlockSpec((tm, K), row))
        args.append(a)
    else:
        o_f, o_b, g_arr, g_col, gain = mix_args
        gw = gain.shape[0]
        in_specs += [pl.BlockSpec((tm, K), row), pl.BlockSpec((tm, K), row),
                     pl.BlockSpec((tm, K), lambda i: (i, g_col)),
                     pl.BlockSpec((1, gw), lambda i: (0, 0))]
        args += [o_f, o_b, g_arr, gain.reshape(1, gw).astype(F32)]
    in_specs += [pl.BlockSpec((K, D_MODEL), lambda i: (0, 0)),
                 pl.BlockSpec((1, D_MODEL), lambda i: (0, 0)),
                 pl.BlockSpec((N_EXPERTS, D_MODEL), lambda i: (0, 0))]
    args += [w_out, norm_g.reshape(1, D_MODEL).astype(F32), w_router_t]
    return pl.pallas_call(
        functools.partial(_outproj_body, kind=kind), grid=(T // tm,), in_specs=in_specs,
        out_specs=[pl.BlockSpec((tm, D_MODEL), row), pl.BlockSpec((tm * SUBLANES, LANES), row),
                   pl.BlockSpec((N_EXPERTS, tm), lambda i: (0, i))],
        out_shape=[jax.ShapeDtypeStruct((T, D_MODEL), F32), jax.ShapeDtypeStruct((T * SUBLANES, LANES), F32),
                   jax.ShapeDtypeStruct((N_EXPERTS, T), F32)],
        compiler_params=_cparams(1), name=name)(*args)


NA_QROWS = 8


def _na_body(q_ref, kp_ref, kc_ref, kn_ref, vp_ref, vc_ref, vn_ref, bt_ref, o_ref, kbuf, vbuf, *, rows):
    rb = pl.program_id(2)
    blk = NA_QROWS * GRID_W
    for n, (kr, vr) in enumerate(((kp_ref, vp_ref), (kc_ref, vc_ref), (kn_ref, vn_ref))):
        kbuf[n * blk:(n + 1) * blk, :] = kr[...]
        vbuf[n * blk:(n + 1) * blk, :] = vr[...]
    lane = lax.broadcasted_iota(I32, (1, LANES), 1)
    lo = lane < 64
    zero = jnp.zeros((), BF16)

    def scores(qi):
        r = rb * NA_QROWS + qi
        r_start = jnp.clip(r - NA_WIN_ROWS // 2, 0, rows - NA_WIN_ROWS)
        roff0 = r_start - r + (NA_WIN_ROWS - 1)
        off = pl.multiple_of((r_start - rb * NA_QROWS + NA_QROWS) * GRID_W, GRID_W)
        kw = kbuf[pl.ds(off, NA_WIN_ROWS * GRID_W), :]
        qp = q_ref[qi * GRID_W:(qi + 1) * GRID_W, :]
        q2 = jnp.concatenate([jnp.where(lo, qp, zero), jnp.where(lo, zero, qp)], axis=0)
        return lax.dot_general(q2, kw, NT, preferred_element_type=F32) + bt_ref[0, roff0], off

    nxt = scores(0)
    for qi in range(NA_QROWS):
        s, off = nxt
        if qi + 1 < NA_QROWS:
            nxt = scores(qi + 1)
        p = jnp.exp2(s - jnp.max(s, axis=-1, keepdims=True))
        l = jnp.sum(p, axis=-1, keepdims=True)
        o = jnp.dot(p.astype(BF16), vbuf[pl.ds(off, NA_WIN_ROWS * GRID_W), :], preferred_element_type=F32) / l
        o_ref[qi * GRID_W:(qi + 1) * GRID_W, :] = jnp.where(lo, o[:GRID_W], o[GRID_W:]).astype(o_ref.dtype)


def _na_bias_table(rel_bias):
    c = jnp.arange(GRID_W)
    c_start = jnp.clip(c - NA_WIN_COLS // 2, 0, GRID_W - NA_WIN_COLS)
    kc = jnp.arange(GRID_W)
    valid = (kc[None, :] >= c_start[:, None]) & (kc[None, :] < c_start[:, None] + NA_WIN_COLS)
    coff = jnp.clip(kc[None, :] - c[:, None] + (NA_WIN_COLS - 1), 0, 2 * NA_WIN_COLS - 2)
    onehot = jnp.logical_and(coff[None] == jnp.arange(2 * NA_WIN_COLS - 1)[:, None, None], valid[None])
    bm = jnp.einsum("hrj,jck->hrck", rel_bias * LOG2E, onehot.astype(F32), precision=lax.Precision.HIGHEST)
    bm = bm + jnp.where(valid, 0.0, NEG)[None, None]
    bt = jnp.stack([jnp.concatenate([bm[:, r0 + i] for i in range(NA_WIN_ROWS)], axis=-1)
                    for r0 in range(NA_WIN_ROWS)], axis=1).astype(F32)
    bt = bt.reshape(NA_HEADS // 2, 2, NA_WIN_ROWS, GRID_W, NA_WIN_ROWS * GRID_W)
    return jnp.swapaxes(bt, 1, 2).reshape(NA_HEADS // 2, NA_WIN_ROWS, 2 * GRID_W, NA_WIN_ROWS * GRID_W)


def _na_attention(qk, v, bt, bsz, seq_len):
    rows = seq_len // GRID_W
    assert rows % NA_QROWS == 0 and rows >= NA_WIN_ROWS
    nrb = rows // NA_QROWS
    blk = NA_QROWS * GRID_W
    npair = NA_HEADS // 2

    def spec(col0, shift):
        return pl.BlockSpec((blk, LANES),
                            lambda hp, b, rb: (b * nrb + jnp.clip(rb + shift, 0, nrb - 1), col0 + hp))
    in_specs = [spec(0, 0), spec(npair, -1), spec(npair, 0), spec(npair, 1),
                spec(0, -1), spec(0, 0), spec(0, 1),
                pl.BlockSpec((1, NA_WIN_ROWS, 2 * GRID_W, NA_WIN_ROWS * GRID_W), lambda hp, b, rb: (hp, 0, 0, 0))]
    return pl.pallas_call(
        functools.partial(_na_body, rows=rows), grid=(npair, bsz, nrb), in_specs=in_specs,
        out_specs=spec(0, 0),
        out_shape=jax.ShapeDtypeStruct((bsz * seq_len, D_MODEL), BF16),
        scratch_shapes=[pltpu.VMEM((3 * blk, LANES), BF16), pltpu.VMEM((3 * blk, LANES), BF16)],
        compiler_params=_cparams(3), name="na_attention")(qk, qk, qk, qk, v, v, v, bt)


HG_CHUNK = 128
HG_BLOCK = 512
HG_PAIR = 2


def _bcast_row(x, group, r):
    C = x.shape[0]
    x3 = x.reshape(C // group, group, x.shape[1])
    return jnp.broadcast_to(x3[:, r:r + 1, :], x3.shape).reshape(x.shape)


def _hgrn_body(xq_ref, xi_ref, xf_ref, lb_ref, o_ref, st_ref, *, rev, nchunk):
    C = HG_CHUNK

    @pl.when(pl.program_id(2) == 0)
    def _():
        st_ref[...] = jnp.zeros_like(st_ref)

    row = lax.broadcasted_iota(I32, (C, C), 0)
    col = lax.broadcasted_iota(I32, (C, C), 1)
    tri = jnp.where((row <= col) if rev else (row >= col), 1.0, 0.0).astype(BF16)
    t_idx = lax.broadcasted_iota(I32, (C, HG_DIM), 0)
    sub = t_idx % SUBLANES

    def one_head(rws, hh):
        cols = slice(hh * HG_DIM, (hh + 1) * HG_DIM)
        lbv = lb_ref[:, cols]
        q = xq_ref[rws, cols] * (HG_DIM ** -0.5)
        xi = xi_ref[rws, cols]
        v = xi * _sigmoid(xi)
        f = lbv + (1.0 - lbv) * _sigmoid(xf_ref[rws, cols])
        k = 1.0 - f
        lf = jnp.log(f) * LOG2E
        lf_hi = lf.astype(BF16)
        lf_lo = (lf - lf_hi.astype(F32)).astype(BF16)
        b = (jnp.dot(tri, lf_hi, preferred_element_type=F32)
             + jnp.dot(tri, lf_lo, preferred_element_type=F32))
        vb = v.astype(BF16)
        state = st_ref[hh]
        o = jnp.dot((q * jnp.exp2(b)).astype(BF16), state.astype(BF16), preferred_element_type=F32)
        for s in range(SUBLANES):
            msk = (sub <= s) if rev else (sub >= s)
            e = jnp.exp2(jnp.where(msk, b - _bcast_row(b, SUBLANES, s), NEG))
            sc = jnp.sum(q * _bcast_row(k, SUBLANES, s) * e, axis=-1, keepdims=True)
            o = o + sc * _bcast_row(v, SUBLANES, s)
        scores = jnp.zeros((C, C), F32)
        m = SUBLANES
        while m < C:
            right = ((t_idx // m) % 2) == 1
            rr = _bcast_row(b, 2 * m, m if rev else m - 1)
            qside = jnp.logical_not(right) if rev else right
            kside = right if rev else jnp.logical_not(right)
            qe = jnp.where(qside, q * jnp.exp2(jnp.where(qside, b - rr, 0.0)), 0.0)
            ke = jnp.where(kside, k * jnp.exp2(jnp.where(kside, rr - b, 0.0)), 0.0)
            sl = lax.dot_general(qe.astype(BF16), ke.astype(BF16), NT, preferred_element_type=F32)
            scores = scores + jnp.where((row // (2 * m)) == (col // (2 * m)), sl, 0.0)
            m *= 2
        o = o + jnp.dot(scores.astype(BF16), vb, preferred_element_type=F32)
        o_ref[rws, cols] = o
        bl = b[0:1, :] if rev else b[C - 1:C, :]
        ke = (k * jnp.exp2(bl - b)).astype(BF16)
        upd = lax.dot_general(ke, vb, TN, preferred_element_type=F32)
        decay = jnp.transpose(jnp.broadcast_to(jnp.exp2(bl), (HG_DIM, HG_DIM)))
        st_ref[hh] = state * decay + upd

    def chunk(ci, carry):
        cc = (nchunk - 1 - ci) if rev else ci
        rws = pl.ds(pl.multiple_of(cc * C, C), C)
        for hh in range(HG_PAIR):
            one_head(rws, hh)
        return carry

    lax.fori_loop(0, nchunk, chunk, 0)


def _hgrn_direction(y, lb, bsz, seq_len, rev):
    lbk = min(HG_BLOCK, seq_len)
    assert seq_len % lbk == 0 and lbk % HG_CHUNK == 0
    nb = seq_len // lbk
    fpart = 3 if rev else 2
    npair = HG_HEADS // HG_PAIR
    width = HG_PAIR * HG_DIM

    def spec(part):
        return pl.BlockSpec((lbk, width),
                            lambda b, h, c: (b * nb + ((nb - 1 - c) if rev else c), part * npair + h))
    return pl.pallas_call(
        functools.partial(_hgrn_body, rev=rev, nchunk=lbk // HG_CHUNK),
        grid=(bsz, npair, nb),
        in_specs=[spec(0), spec(1), spec(fpart), pl.BlockSpec((1, width), lambda b, h, c: (0, h))],
        out_specs=spec(0),
        out_shape=jax.ShapeDtypeStruct((bsz * seq_len, D_MODEL), F32),
        scratch_shapes=[pltpu.VMEM((HG_PAIR, HG_DIM, HG_DIM), F32)],
        compiler_params=_cparams(3), name="hgrn_bwd" if rev else "hgrn_fwd")(y, y, y, lb.reshape(1, D_MODEL))


def _flash_body(q_ref, k_ref, v_ref, o_ref, m_sc, acc_sc):
    ki = pl.program_id(3)

    @pl.when(ki == 0)
    def _():
        m_sc[...] = jnp.full_like(m_sc, NEG)
        acc_sc[...] = jnp.zeros_like(acc_sc)

    lane = lax.broadcasted_iota(I32, (1, LANES), 1)
    lo = lane < 64
    zero = jnp.zeros((), BF16)
    k = k_ref[...]
    v1 = jnp.where(lo, v_ref[...], jnp.ones((), BF16))

    def scores(hd):
        qp = q_ref[:, (hd // 2) * LANES:(hd // 2 + 1) * LANES]
        qm = jnp.where(lo if hd % 2 == 0 else jnp.logical_not(lo), qp, zero)
        return lax.dot_general(qm, k, NT, preferred_element_type=F32)

    n_heads = GQA_HEADS // GQA_KV
    s_next = scores(0)
    for hd in range(n_heads):
        s = s_next
        if hd + 1 < n_heads:
            s_next = scores(hd + 1)
        m_prev = m_sc[hd]
        m_new = jnp.maximum(m_prev, jnp.max(s, axis=-1, keepdims=True))
        p = jnp.exp2(s - m_new)
        acc_sc[hd] = (jnp.exp2(m_prev - m_new) * acc_sc[hd]
                      + jnp.dot(p.astype(BF16), v1, preferred_element_type=F32))
        m_sc[hd] = m_new

    @pl.when(ki == pl.num_programs(3) - 1)
    def _():
        for pair in range(2):
            a0 = acc_sc[2 * pair]
            a1 = pltpu.roll(acc_sc[2 * pair + 1], 64, 1)
            o_ref[:, pair * LANES:(pair + 1) * LANES] = jnp.where(
                lo, a0 / a0[:, 64:65], a1 / a1[:, 0:1]).astype(o_ref.dtype)


def _flash_gqa(qk, v, bsz, seq_len):
    tq = min(256, seq_len)
    tk = min(8192, seq_len)
    nq, nk = seq_len // tq, seq_len // tk
    gw = (GQA_HEADS // GQA_KV) * GQA_DH
    return pl.pallas_call(
        _flash_body, grid=(bsz, GQA_KV, nq, nk),
        in_specs=[pl.BlockSpec((tq, gw), lambda b, g, i, j: (b * nq + i, g)),
                  pl.BlockSpec((tk, LANES), lambda b, g, i, j: (b * nk + j, D_MODEL // LANES + g)),
                  pl.BlockSpec((tk, LANES), lambda b, g, i, j: (b * nk + j, g))],
        out_specs=pl.BlockSpec((tq, gw), lambda b, g, i, j: (b * nq + i, g)),
        out_shape=jax.ShapeDtypeStruct((bsz * seq_len, D_MODEL), BF16),
        scratch_shapes=[pltpu.VMEM((4, tq, 1), F32), pltpu.VMEM((4, tq, LANES), F32)],
        compiler_params=_cparams(4), name="flash_gqa")(qk, qk, v)


RET_CHUNK = 128
RET_BLOCK = 512


def _ret_body(q_ref, k_ref, v_ref, dm_ref, qd_ref, kd_ref, cd_ref, o_ref, st_ref, *, rev, nchunk):
    C = RET_CHUNK

    @pl.when(pl.program_id(2) == 0)
    def _():
        st_ref[...] = jnp.zeros_like(st_ref)

    def chunk(ci, carry):
        cc = (nchunk - 1 - ci) if rev else ci
        rws = pl.ds(pl.multiple_of(cc * C, C), C)
        q = q_ref[rws, :]
        k = k_ref[rws, :]
        v = v_ref[rws, :]
        state = st_ref[...]
        s = lax.dot_general(q, k, NT, preferred_element_type=F32) * dm_ref[0]
        o = jnp.dot(s.astype(BF16), v, preferred_element_type=F32)
        qs = (q.astype(F32) * qd_ref[0]).astype(BF16)
        o = o + jnp.dot(qs, state.astype(BF16), preferred_element_type=F32)
        o_ref[rws, :] = o
        ks = (k.astype(F32) * kd_ref[0]).astype(BF16)
        st_ref[...] = state * cd_ref[0] + lax.dot_general(ks, v, TN, preferred_element_type=F32)
        return carry

    lax.fori_loop(0, nchunk, chunk, 0)


def _ret_tables(rev):
    C = RET_CHUNK
    log_gamma = jnp.log1p(-jnp.exp2(-5.0 - jnp.arange(RET_HEADS, dtype=F32)))[:, None, None]
    pos = jnp.arange(C, dtype=F32)
    rel = pos[:, None] - pos[None, :]
    if rev:
        rel = -rel
        qpow, kpow = C - pos, pos
    else:
        qpow, kpow = pos + 1.0, C - 1.0 - pos
    dm = jnp.where(rel[None] >= 0, jnp.exp(rel[None] * log_gamma), 0.0)
    qd = jnp.broadcast_to(jnp.exp(qpow[None, :, None] * log_gamma), (RET_HEADS, C, RET_DK))
    kd = jnp.broadcast_to(jnp.exp(kpow[None, :, None] * log_gamma), (RET_HEADS, C, RET_DK))
    cd = jnp.broadcast_to(jnp.exp(C * log_gamma), (RET_HEADS, 1, RET_DV))
    return dm.astype(F32), qd.astype(F32), kd.astype(F32), cd.astype(F32)


def _ret_direction(qk, v, bsz, seq_len, rev):
    lbk = min(RET_BLOCK, seq_len)
    nb = seq_len // lbk
    rowblk = lambda b, h, c: b * nb + ((nb - 1 - c) if rev else c)
    tab = lambda shape: pl.BlockSpec((1,) + shape, lambda b, h, c: (h, 0, 0))
    return pl.pallas_call(
        functools.partial(_ret_body, rev=rev, nchunk=lbk // RET_CHUNK),
        grid=(bsz, RET_HEADS, nb),
        in_specs=[pl.BlockSpec((lbk, RET_DK), lambda b, h, c: (rowblk(b, h, c), h)),
                  pl.BlockSpec((lbk, RET_DK), lambda b, h, c: (rowblk(b, h, c), RET_HEADS + h)),
                  pl.BlockSpec((lbk, RET_DV), lambda b, h, c: (rowblk(b, h, c), h)),
                  tab((RET_CHUNK, RET_CHUNK)), tab((RET_CHUNK, RET_DK)), tab((RET_CHUNK, RET_DK)),
                  tab((1, RET_DV))],
        out_specs=pl.BlockSpec((lbk, RET_DV), lambda b, h, c: (rowblk(b, h, c), h)),
        out_shape=jax.ShapeDtypeStruct((bsz * seq_len, RET_HEADS * RET_DV), F32),
        scratch_shapes=[pltpu.VMEM((RET_DK, RET_DV), F32)],
        compiler_params=_cparams(3), name="ret_bwd" if rev else "ret_fwd")(qk, qk, v, *_ret_tables(rev))


MOE_FF_CHUNK = 512


MOE_SLOTS = 3


def _moe_body(idx_ref, nx1_ref, nx2_ref, u_hbm, gate_ref, wg_ref, wu_ref, wd_ref, hi_ref, lo_ref, xbuf, sem, *, tc):
    nsteps = pl.num_programs(0) * pl.num_programs(1)
    step = pl.program_id(0) * pl.num_programs(1) + pl.program_id(1)
    slot = step % MOE_SLOTS

    def row_copy(token, r, s):
        src = u_hbm.at[pl.ds(pl.multiple_of(token * SUBLANES, SUBLANES), SUBLANES)]
        return pltpu.make_async_copy(src, xbuf.at[s, pl.ds(r * SUBLANES, SUBLANES)], sem.at[s])

    def wait_rows(s):
        for _ in range(tc):
            row_copy(0, 0, s).wait()

    @pl.when(step == 0)
    def _():
        def issue(r, carry):
            row_copy(idx_ref[0, 0, r], r, 0).start()
            row_copy(nx1_ref[0, 0, r], r, 1).start()
            return carry
        lax.fori_loop(0, tc, issue, 0)

    wait_rows(slot)
    x = jnp.concatenate([xbuf[slot, pl.ds(s, tc, stride=SUBLANES), :] for s in range(SUBLANES)],
                        axis=1).astype(BF16)
    ahead = (step + 2) % MOE_SLOTS
    for r in range(tc):
        row_copy(nx2_ref[0, 0, r], r, ahead).start()
    acc = jnp.zeros((tc, D_MODEL), F32)
    for f in range(EXPERT_FF // MOE_FF_CHUNK):
        cols = slice(f * MOE_FF_CHUNK, (f + 1) * MOE_FF_CHUNK)
        g = jnp.dot(x, wg_ref[0, 0, :, cols], preferred_element_type=F32)
        up = jnp.dot(x, wu_ref[0, 0, :, cols], preferred_element_type=F32)
        hid = ((g * _sigmoid(g)) * up).astype(BF16)
        acc = acc + jnp.dot(hid, wd_ref[0, 0, cols, :], preferred_element_type=F32)
    gate = gate_ref[0]
    for c in range(D_MODEL // LANES):
        cols = slice(c * LANES, (c + 1) * LANES)
        ye = acc[:, cols] * gate
        hi = ye.astype(BF16)
        hi_ref[0, :, cols] = hi
        lo_ref[0, :, cols] = (ye - hi.astype(F32)).astype(BF16)

    @pl.when(step == nsteps - 1)
    def _():
        wait_rows((step + 1) % MOE_SLOTS)
        wait_rows(ahead)


def _moe_experts(u, idx_sorted, gate_sorted, wg, wu, wd, layer):
    cap = idx_sorted.shape[1]
    tc = min(512, cap)
    nj = cap // tc
    nsteps = N_EXPERTS * nj
    assert nsteps >= MOE_SLOTS
    idx3 = idx_sorted.reshape(nsteps, 1, tc)
    gate_b = jnp.broadcast_to(gate_sorted[:, :, None], (N_EXPERTS, cap, LANES))
    ispec = lambda d: pl.BlockSpec((1, 1, tc), lambda e, j: (jnp.minimum(e * nj + j + d, nsteps - 1), 0, 0),
                                   memory_space=pltpu.SMEM)
    wspec = lambda shape: pl.BlockSpec((1, 1) + shape, lambda e, j: (layer, e, 0, 0))
    ospec = pl.BlockSpec((1, tc, D_MODEL), lambda e, j: (e, j, 0))
    oshape = jax.ShapeDtypeStruct((N_EXPERTS, cap, D_MODEL), BF16)
    return pl.pallas_call(
        functools.partial(_moe_body, tc=tc), grid=(N_EXPERTS, nj),
        in_specs=[ispec(0), ispec(1), ispec(2),
                  pl.BlockSpec(memory_space=pl.ANY),
                  pl.BlockSpec((1, tc, LANES), lambda e, j: (e, j, 0)),
                  wspec((D_MODEL, EXPERT_FF)), wspec((D_MODEL, EXPERT_FF)), wspec((EXPERT_FF, D_MODEL))],
        out_specs=[ospec, ospec], out_shape=[oshape, oshape],
        scratch_shapes=[pltpu.VMEM((MOE_SLOTS, tc * SUBLANES, LANES), F32),
                        pltpu.SemaphoreType.DMA((MOE_SLOTS,))],
        compiler_params=_cparams(2), name="moe_experts")(idx3, idx3, idx3, u, gate_b, wg, wu, wd)


CMB_TOKENS = 256
CMB_WINDOW = LANES // 2
CMB_ALIGN = 2 * SUBLANES


def _combine_body(base_ref, nr_ref, h_ref, pos_ref, hi_hbm, lo_hbm, o_ref, buf, xbuf, sem, xsem, *, cap, ntiles):
    i = pl.program_id(0)
    slot = i % 2
    Wn = CMB_WINDOW
    lane = lax.broadcasted_iota(I32, (1, LANES), 1)
    lo_half = lane < Wn

    def window(tile, e, k):
        start = (base_ref[e * ntiles + tile] // CMB_ALIGN) * CMB_ALIGN + k * Wn
        w0 = pl.multiple_of(jnp.minimum(start, cap - Wn), CMB_ALIGN)
        return start, w0

    def copies(tile, k, dst, dsem):
        out = []
        for e in range(N_EXPERTS):
            w0 = window(tile, e, k)[1]
            out.append(pltpu.make_async_copy(hi_hbm.at[e, pl.ds(w0, Wn)], dst.at[0, e], dsem))
            out.append(pltpu.make_async_copy(lo_hbm.at[e, pl.ds(w0, Wn)], dst.at[1, e], dsem))
        return out

    def one_hot(k, check_start):
        parts = []
        for e in range(0, N_EXPERTS, 2):
            s0, w0 = window(i, e, k)
            s1, w1 = window(i, e + 1, k)
            p0 = pos_ref[:, e:e + 1]
            p1 = pos_ref[:, e + 1:e + 2]
            hit = jnp.where(lo_half, p0 - w0, p1 - w1 + Wn) == lane
            if check_start:
                hit = jnp.logical_and(hit, jnp.where(lo_half, p0 - s0, p1 - s1) >= 0)
            parts.append(jnp.where(hit, 1.0, 0.0).astype(BF16))
        return jnp.concatenate(parts, axis=1)

    def placed(smat, src):
        return (jnp.dot(smat, src[0].reshape(N_EXPERTS * Wn, D_MODEL), preferred_element_type=F32)
                + jnp.dot(smat, src[1].reshape(N_EXPERTS * Wn, D_MODEL), preferred_element_type=F32))

    @pl.when(i == 0)
    def _():
        for c in copies(0, 0, buf.at[0], sem.at[0]):
            c.start()

    @pl.when(i + 1 < ntiles)
    def _():
        for c in copies(i + 1, 0, buf.at[1 - slot], sem.at[1 - slot]):
            c.start()

    smat = one_hot(0, False)
    for c in copies(i, 0, buf.at[slot], sem.at[slot]):
        c.wait()
    acc = h_ref[...] + placed(smat, buf[slot])

    def extra_round(k, acc):
        for c in copies(i, k, xbuf, xsem):
            c.start()
        smat = one_hot(k, True)
        for c in copies(i, k, xbuf, xsem):
            c.wait()
        return acc + placed(smat, xbuf[...])

    o_ref[...] = lax.fori_loop(1, nr_ref[i], extra_round, acc)


def _moe_combine(h, ye_hi, ye_lo, pos_t, base, nrounds):
    T = h.shape[0]
    cap = ye_hi.shape[1]
    ntiles = T // CMB_TOKENS
    assert cap >= CMB_WINDOW and cap % CMB_ALIGN == 0
    wshape = (2, N_EXPERTS, CMB_WINDOW, D_MODEL)
    grid_spec = pltpu.PrefetchScalarGridSpec(
        num_scalar_prefetch=2, grid=(ntiles,),
        in_specs=[pl.BlockSpec((CMB_TOKENS, D_MODEL), lambda i, b, n: (i, 0)),
                  pl.BlockSpec((CMB_TOKENS, N_EXPERTS), lambda i, b, n: (i, 0)),
                  pl.BlockSpec(memory_space=pl.ANY), pl.BlockSpec(memory_space=pl.ANY)],
        out_specs=pl.BlockSpec((CMB_TOKENS, D_MODEL), lambda i, b, n: (i, 0)),
        scratch_shapes=[pltpu.VMEM((2,) + wshape, BF16), pltpu.VMEM(wshape, BF16),
                        pltpu.SemaphoreType.DMA((2,)), pltpu.SemaphoreType.DMA(())])
    return pl.pallas_call(
        functools.partial(_combine_body, cap=cap, ntiles=ntiles), grid_spec=grid_spec,
        out_shape=jax.ShapeDtypeStruct((T, D_MODEL), F32),
        compiler_params=_cparams(1), name="moe_combine")(base.reshape(-1), nrounds, h, pos_t, ye_hi, ye_lo)


def _flag(cond):
    return jnp.where(cond, 1.0, 0.0)


def _select_body(aff_ref, pos_ref, idx_ref, gate_ref, thr_sc, *, cap):
    n_exp, n_chunk, _ = aff_ref.shape

    def bit_step(i, prefix):
        cand = prefix | jnp.left_shift(jnp.int32(1), 30 - i)
        hit = _flag(pltpu.bitcast(aff_ref[...], I32) >= cand)
        cnt = jnp.sum(jnp.sum(hit, axis=1, keepdims=True), axis=2, keepdims=True)
        return jnp.where(cnt >= cap, cand, prefix)

    thr = lax.fori_loop(0, 31, bit_step, jnp.zeros((n_exp, 1, 1), I32))
    thr_sc[...] = jnp.broadcast_to(thr, thr_sc.shape)

    li = lax.broadcasted_iota(I32, (LANES, LANES), 0)
    lj = lax.broadcasted_iota(I32, (LANES, LANES), 1)
    upper = _flag(li <= lj).astype(BF16)
    ci = lax.broadcasted_iota(I32, (n_chunk, n_chunk), 0)
    cj = lax.broadcasted_iota(I32, (n_chunk, n_chunk), 1)
    before = _flag(cj < ci).astype(BF16)
    chunk_id = lax.broadcasted_iota(I32, (n_chunk, LANES), 0).astype(F32)
    lane_id = lax.broadcasted_iota(I32, (LANES, LANES), 0).astype(F32)
    slot_lane = lax.broadcasted_iota(I32, (1, LANES), 1)

    def counts(flags):
        local = jnp.dot(flags.astype(BF16), upper, preferred_element_type=F32)
        total = jnp.broadcast_to(local[:, LANES - 1:LANES], local.shape)
        return local, total, jnp.dot(before, total.astype(BF16), preferred_element_type=F32)

    def per_expert(e, carry):
        aff = aff_ref[e]
        key = pltpu.bitcast(aff, I32)
        t = thr_sc[e]
        gt = _flag(key > t)
        eq = _flag(key == t)
        need = cap - jnp.sum(jnp.sum(gt, axis=1, keepdims=True), axis=0, keepdims=True)
        eq_local, _, eq_off = counts(eq)
        sel = gt + eq * _flag(eq_local + eq_off - eq < need)
        local, total, off = counts(sel)
        pos_ref[e] = jnp.where(sel > 0.0, local + off - 1.0, -1.0).astype(I32)
        reached = off + total
        local_t = jnp.transpose(local).astype(BF16)
        aff_tr = jnp.transpose(aff)

        def per_row(r, c2):
            slot = (r * LANES + slot_lane).astype(F32)
            chunk = jnp.sum(_flag(reached <= slot), axis=0, keepdims=True)
            skipped = jnp.sum(jnp.where(chunk_id < chunk, total, 0.0), axis=0, keepdims=True)
            pick = _flag(chunk_id == chunk)
            run = jnp.dot(local_t, pick.astype(BF16), preferred_element_type=F32)
            lane = jnp.sum(_flag(run <= slot - skipped), axis=0, keepdims=True)
            idx_ref[e, pl.ds(r, 1), :] = (chunk * LANES + lane).astype(I32)
            vals = jnp.dot(aff_tr, pick, precision=lax.Precision.HIGHEST, preferred_element_type=F32)
            gate_ref[e, pl.ds(r, 1), :] = jnp.sum(jnp.where(lane_id == lane, vals, 0.0), axis=0, keepdims=True)
            return c2

        lax.fori_loop(0, cap // LANES, per_row, 0)
        return carry

    lax.fori_loop(0, n_exp, per_expert, 0)


def _select(aff_t, cap):
    E, T = aff_t.shape
    assert T % LANES == 0 and cap % LANES == 0
    full = lambda shape: pl.BlockSpec(shape, lambda i: (0, 0, 0))
    shapes = [(E, T // LANES, LANES), (E, cap // LANES, LANES), (E, cap // LANES, LANES)]
    pos, idx, gate = pl.pallas_call(
        functools.partial(_select_body, cap=cap), grid=(1,),
        in_specs=[full(shapes[0])], out_specs=[full(s) for s in shapes],
        out_shape=[jax.ShapeDtypeStruct(shapes[0], I32), jax.ShapeDtypeStruct(shapes[1], I32),
                   jax.ShapeDtypeStruct(shapes[2], F32)],
        scratch_shapes=[pltpu.VMEM((E, 1, LANES), I32)],
        compiler_params=_cparams(1), name="moe_select")(aff_t.reshape(shapes[0]))
    return pos.reshape(E, T), idx.reshape(E, cap), gate.reshape(E, cap)


def _route(aff_t, cap):
    E, T = aff_t.shape
    ntiles = T // CMB_TOKENS
    pos, idx_s, gate_s = _select(aff_t, cap)
    cnt = jnp.sum((pos >= 0).reshape(E, ntiles, CMB_TOKENS), axis=-1, dtype=I32)
    base = jnp.cumsum(cnt, axis=1, dtype=I32) - cnt
    need = jnp.where(cnt > 0, (base % CMB_ALIGN + cnt + CMB_WINDOW - 1) // CMB_WINDOW, 0)
    return idx_s, gate_s, pos.T, base, jnp.max(need, axis=0).astype(I32)


def _moe_layer(h, u, aff_t, wg, wu, wd, layer):
    T = h.shape[0]
    cap = CAPACITY_FACTOR * T // N_EXPERTS
    idx_s, gate_s, pos_t, base, nrounds = _route(aff_t, cap)
    ye_hi, ye_lo = _moe_experts(u, idx_s, gate_s, wg, wu, wd, layer)
    return _moe_combine(h, ye_hi, ye_lo, pos_t, base, nrounds)


def _rope_tables(seq_len, n_freq, reps):
    t = jnp.arange(seq_len)
    inv_freq = ROPE_THETA ** (-jnp.arange(n_freq, dtype=F32) / n_freq)
    out = []
    for pos in ((t // GRID_W).astype(F32), (t % GRID_W).astype(F32)):
        ang = pos[:, None] * inv_freq[None, :]
        c = jnp.concatenate([jnp.cos(ang), jnp.cos(ang)], axis=1)
        s = jnp.concatenate([-jnp.sin(ang), jnp.sin(ang)], axis=1)
        out.append((c, s))
    cos = jnp.concatenate([out[0][0], out[1][0]], axis=1)
    sin = jnp.concatenate([out[0][1], out[1][1]], axis=1)
    return jnp.tile(cos, (1, reps)), jnp.tile(sin, (1, reps))


def _dup_heads(w, n_heads, dh):
    k = w.shape[0]
    return jnp.repeat(w.reshape(k, n_heads, 1, dh), 2, axis=2).reshape(k, n_heads * 2 * dh)


def _trunk(x, p):
    bsz, seq_len, _ = x.shape
    T = bsz * seq_len
    h = x.reshape(T, D_MODEL)
    for layer in range(4):
        nm = p["norm_mix"][layer]
        if layer == 0:
            qk = _proj(h, nm, p["na_wqk"], seq_len, BF16, headnorm=True, colvec=p["na_qk_gain"], name="na_proj_qk")
            v = _proj(h, nm, p["na_wv"], seq_len, BF16, name="na_proj_v")
            mix_args = (_na_attention(qk, v, p["na_bt"], bsz, seq_len),)
            kind, w_out = "plain", p["na_wo"]
        elif layer == 1:
            y = _proj(h, nm, p["hg_win"], seq_len, F32, name="hg_proj")
            o_f = _hgrn_direction(y, p["hg_lb"], bsz, seq_len, False)
            o_b = _hgrn_direction(y, p["hg_lb"], bsz, seq_len, True)
            mix_args = (o_f, o_b, y, 4, p["hg_o_gain"])
            kind, w_out = "hg", p["hg_wo"]
        elif layer == 2:
            cos, sin = _rope_tables(seq_len, 16, 2)
            qk = _proj(h, nm, p["gq_wqk"], seq_len, BF16, headnorm=True, colvec=p["gq_qk_gain"],
                       rope=(16, cos, sin), name="gqa_proj_qk")
            v = _proj(h, nm, p["gq_wv"], seq_len, BF16, name="gqa_proj_v")
            mix_args = (_flash_gqa(qk, v, bsz, seq_len),)
            kind, w_out = "plain", p["gq_wo"]
        else:
            cos, sin = _rope_tables(seq_len, 64, 1)
            qk = _proj(h, nm, p["rt_wqk"], seq_len, BF16, colvec=p["rt_qk_scale"],
                       rope=(64, cos, sin), name="ret_proj_qk")
            v = _proj(h, nm, p["rt_wv"], seq_len, BF16, name="ret_proj_v")
            g = _proj(h, nm, p["rt_wg"], seq_len, F32, name="ret_proj_g")
            o_f = _ret_direction(qk, v, bsz, seq_len, False)
            o_b = _ret_direction(qk, v, bsz, seq_len, True)
            mix_args = (o_f, o_b, g, 0, p["rt_o_gain"])
            kind, w_out = "rt", p["rt_wo"]
        h, u, aff_t = _outproj(h, mix_args, w_out, p["norm_ffn"][layer], p["router_t"][layer], kind,
                               name=f"outproj_{kind}")
        h = _moe_layer(h, u, aff_t, p["moe_wg"], p["moe_wu"], p["moe_wd"], layer)
    return h.reshape(bsz, seq_len, D_MODEL)


def kernel(x_prompt, x_sample, norm_mix, norm_ffn, na_w_in, na_q_gain, na_k_gain, na_rel_bias, na_w_out, hg_w_in, hg_lb, hg_o_gain, hg_w_out, gq_w_in, gq_q_gain, gq_k_gain, gq_w_out, rt_w_in, rt_o_gain, rt_w_out, moe_router, moe_w_gate, moe_w_up, moe_w_down):
    bf = lambda a: a.astype(BF16)
    na_w, gq_w, rt_w = na_w_in[0], gq_w_in[0], rt_w_in[0]
    lb_cum = jnp.cumsum(jax.nn.softmax(hg_lb.astype(F32), axis=0), axis=0)
    qd = GQA_HEADS * GQA_DH
    kd = GQA_KV * GQA_DH
    rq = RET_HEADS * RET_DK
    rv = RET_HEADS * RET_DV
    p = {
        "norm_mix": norm_mix, "norm_ffn": norm_ffn,
        "na_wqk": bf(na_w[:, :2 * D_MODEL]), "na_wv": bf(na_w[:, 2 * D_MODEL:]),
        "na_qk_gain": jnp.concatenate([jnp.tile(na_q_gain[0], NA_HEADS) * (64 ** -0.5 * LOG2E),
                                       jnp.tile(na_k_gain[0], NA_HEADS)]),
        "na_bt": _na_bias_table(na_rel_bias[0]), "na_wo": bf(na_w_out[0]),
        "hg_win": bf(hg_w_in[0]), "hg_lb": lb_cum[1] - lb_cum[0], "hg_o_gain": hg_o_gain[0],
        "hg_wo": bf(hg_w_out[0]),
        "gq_wqk": bf(jnp.concatenate([gq_w[:, :qd], _dup_heads(gq_w[:, qd:qd + kd], GQA_KV, GQA_DH)], axis=1)),
        "gq_wv": bf(_dup_heads(gq_w[:, qd + kd:], GQA_KV, GQA_DH)),
        "gq_qk_gain": jnp.concatenate([jnp.tile(gq_q_gain[0], GQA_HEADS) * (GQA_DH ** -0.5 * LOG2E),
                                       jnp.tile(gq_k_gain[0], 2 * GQA_KV)]),
        "gq_wo": bf(gq_w_out[0]),
        "rt_wqk": bf(rt_w[:, :2 * rq]), "rt_wv": bf(rt_w[:, 2 * rq:2 * rq + rv]), "rt_wg": bf(rt_w[:, 2 * rq + rv:]),
        "rt_qk_scale": jnp.concatenate([jnp.full((rq,), RET_DK ** -0.5, F32), jnp.ones((rq,), F32)]),
        "rt_o_gain": rt_o_gain[0], "rt_wo": bf(rt_w_out[0]),
        "router_t": jnp.swapaxes(moe_router, 1, 2).astype(F32),
        "moe_wg": bf(moe_w_gate), "moe_wu": bf(moe_w_up), "moe_wd": bf(moe_w_down),
    }
    return (_trunk(x_prompt, p), _trunk(x_sample, p))
```

```python
import functools
import math

import jax
import jax.numpy as jnp
from jax import lax
from jax.experimental import pallas as pl
from jax.experimental.pallas import tpu as pltpu

F32 = jnp.float32
BF16 = jnp.bfloat16
I32 = jnp.int32

D_MODEL = 1024
GRID_W = 64
EPS = 1e-6
ROPE_THETA = 10000.0
NEG = -1e30
LOG2E = math.log2(math.e)

NA_HEADS = 16
NA_WIN_ROWS = 8
NA_WIN_COLS = 16
HG_HEADS = 8
HG_DIM = 128
GQA_HEADS = 16
GQA_KV = 4
GQA_DH = 64
RET_HEADS = 4
RET_DK = 256
RET_DV = 512
N_EXPERTS = 16
EXPERT_FF = 2048
CAPACITY_FACTOR = 2

LANES = 128
SUBLANES = 8
VMEM_LIMIT = 52 * 1024 * 1024

NT = (((1,), (1,)), ((), ()))
TN = (((0,), (0,)), ((), ()))


def _cparams(n_axes):
    return pltpu.CompilerParams(dimension_semantics=("arbitrary",) * n_axes,
                                vmem_limit_bytes=VMEM_LIMIT)


def _sigmoid(x):
    return 1.0 / (1.0 + jnp.exp(-x))


def _proj_body(*refs, headnorm, has_cv, rope_k, tw):
    it = iter(refs)
    x_ref, g_ref, w_ref = next(it), next(it), next(it)
    cv_ref = next(it) if has_cv else None
    cos_ref = next(it) if rope_k else None
    sin_ref = next(it) if rope_k else None
    o_ref, xn_ref = next(it), next(it)

    @pl.when(pl.program_id(1) == 0)
    def _():
        x = x_ref[...]
        ms = jnp.mean(x * x, axis=-1, keepdims=True)
        xn_ref[...] = (x * lax.rsqrt(ms + EPS) * g_ref[...]).astype(BF16)

    acc = jnp.dot(xn_ref[...], w_ref[pl.program_id(1)], preferred_element_type=F32)
    if not (headnorm or has_cv or rope_k):
        o_ref[...] = acc.astype(o_ref.dtype)
        return
    lane = lax.broadcasted_iota(I32, (1, LANES), 1)
    lo = lane < 64
    for c in range(acc.shape[1] // LANES):
        cols = slice(c * LANES, (c + 1) * LANES)
        a = acc[:, cols]
        if headnorm:
            sq = a * a
            s_lo = jnp.sum(jnp.where(lo, sq, 0.0), axis=-1, keepdims=True)
            s_hi = jnp.sum(jnp.where(lo, 0.0, sq), axis=-1, keepdims=True)
            a = a * lax.rsqrt(jnp.where(lo, s_lo, s_hi) * (1.0 / 64.0) + EPS)
        if has_cv:
            a = a * cv_ref[:, cols]
        if rope_k:
            tcol = (c * LANES) % tw
            if rope_k == 64:
                partner = pltpu.roll(a, 64, 1)
            else:
                partner = jnp.where((lane & rope_k) != 0, pltpu.roll(a, rope_k, 1),
                                    pltpu.roll(a, LANES - rope_k, 1))
            a = a * cos_ref[:, tcol:tcol + LANES] + partner * sin_ref[:, tcol:tcol + LANES]
        o_ref[:, cols] = a.astype(o_ref.dtype)


def _proj(x, g, w, seq_len, out_dtype, *, headnorm=False, colvec=None, rope=None, name="proj"):
    T = x.shape[0]
    N = w.shape[1]
    tm = min(1024, seq_len)
    tn = 512 if N % 512 == 0 else 256
    assert T % tm == 0 and seq_len % tm == 0 and N % tn == 0
    nsb = seq_len // tm
    in_specs = [pl.BlockSpec((tm, D_MODEL), lambda i, j: (i, 0)),
                pl.BlockSpec((1, D_MODEL), lambda i, j: (0, 0)),
                pl.BlockSpec((N // tn, D_MODEL, tn), lambda i, j: (0, 0, 0))]
    w_tiles = jnp.swapaxes(w.reshape(D_MODEL, N // tn, tn), 0, 1)
    args = [x, g.reshape(1, D_MODEL).astype(F32), w_tiles]
    if colvec is not None:
        in_specs.append(pl.BlockSpec((1, tn), lambda i, j: (0, j)))
        args.append(colvec.reshape(1, N).astype(F32))
    rope_k, tw = 0, LANES
    if rope is not None:
        rope_k, cos, sin = rope
        tw = cos.shape[1]
        assert tn % tw == 0
        in_specs += [pl.BlockSpec((tm, tw), lambda i, j: (i % nsb, 0))] * 2
        args += [cos, sin]
    body = functools.partial(_proj_body, headnorm=headnorm, has_cv=colvec is not None,
                             rope_k=rope_k, tw=tw)
    return pl.pallas_call(
        body, grid=(T // tm, N // tn), in_specs=in_specs,
        out_specs=pl.BlockSpec((tm, tn), lambda i, j: (i, j)),
        out_shape=jax.ShapeDtypeStruct((T, N), out_dtype),
        scratch_shapes=[pltpu.VMEM((tm, D_MODEL), BF16)],
        compiler_params=_cparams(2), name=name)(*args)


def _group_rms(o, gain_ref, width):
    parts = []
    for c in range(o.shape[1] // width):
        a = o[:, c * width:(c + 1) * width]
        ms = jnp.mean(a * a, axis=-1, keepdims=True)
        parts.append(a * lax.rsqrt(ms + EPS) * gain_ref[...])
    return jnp.concatenate(parts, axis=1)


def _outproj_body(*refs, kind):
    it = iter(refs)
    h_ref = next(it)
    if kind == "plain":
        a = next(it)[...]
    else:
        of_ref, ob_ref, g_ref, gain_ref = next(it), next(it), next(it), next(it)
        o = of_ref[...] + ob_ref[...]
        g = g_ref[...]
        if kind == "hg":
            a = (_group_rms(o, gain_ref, HG_DIM) * _sigmoid(g)).astype(BF16)
        else:
            a = ((g * _sigmoid(g)) * _group_rms(o, gain_ref, RET_DV)).astype(BF16)
    w_ref, nf_ref, wr_ref = next(it), next(it), next(it)
    hn_ref, u_ref, aff_ref = next(it), next(it), next(it)
    hn = h_ref[...] + jnp.dot(a, w_ref[...], preferred_element_type=F32)
    hn_ref[...] = hn
    ms = jnp.mean(hn * hn, axis=-1, keepdims=True)
    u = hn * lax.rsqrt(ms + EPS) * nf_ref[...]
    tm = u.shape[0]
    for s in range(SUBLANES):
        u_ref[pl.ds(s, tm, stride=SUBLANES), :] = u[:, s * LANES:(s + 1) * LANES]
    logits = lax.dot_general(wr_ref[...], u, NT, precision=lax.Precision.HIGHEST,
                             preferred_element_type=F32)
    m = jnp.max(logits, axis=0, keepdims=True)
    e = jnp.exp(logits - m)
    aff_ref[...] = e / jnp.sum(e, axis=0, keepdims=True)


def _outproj(h, mix_args, w_out, norm_g, w_router_t, kind, name):
    T = h.shape[0]
    tm = 256 if kind == "rt" else 512
    K = w_out.shape[0]
    row = lambda i: (i, 0)
    in_specs = [pl.BlockSpec((tm, D_MODEL), row)]
    args = [h]
    if kind == "plain":
        (a,) = mix_args
        in_specs.append(pl.BlockSpec((tm, K), row))
        args.append(a)
    else:
        o_f, o_b, g_arr, g_col, gain = mix_args
        gw = gain.shape[0]
        in_specs += [pl.BlockSpec((tm, K), row), pl.BlockSpec((tm, K), row),
                     pl.BlockSpec((tm, K), lambda i: (i, g_col)),
                     pl.BlockSpec((1, gw), lambda i: (0, 0))]
        args += [o_f, o_b, g_arr, gain.reshape(1, gw).astype(F32)]
    in_specs += [pl.BlockSpec((K, D_MODEL), lambda i: (0, 0)),
                 pl.BlockSpec((1, D_MODEL), lambda i: (0, 0)),
                 pl.BlockSpec((N_EXPERTS, D_MODEL), lambda i: (0, 0))]
    args += [w_out, norm_g.reshape(1, D_MODEL).astype(F32), w_router_t]
    return pl.pallas_call(
        functools.partial(_outproj_body, kind=kind), grid=(T // tm,), in_specs=in_specs,
        out_specs=[pl.BlockSpec((tm, D_MODEL), row), pl.BlockSpec((tm * SUBLANES, LANES), row),
                   pl.BlockSpec((N_EXPERTS, tm), lambda i: (0, i))],
        out_shape=[jax.ShapeDtypeStruct((T, D_MODEL), F32), jax.ShapeDtypeStruct((T * SUBLANES, LANES), F32),
                   jax.ShapeDtypeStruct((N_EXPERTS, T), F32)],
        compiler_params=_cparams(1), name=name)(*args)


NA_QROWS = 8


def _na_body(q_ref, kp_ref, kc_ref, kn_ref, vp_ref, vc_ref, vn_ref, bt_ref, o_ref, kbuf, vbuf, *, rows):
    rb = pl.program_id(2)
    blk = NA_QROWS * GRID_W
    for n, (kr, vr) in enumerate(((kp_ref, vp_ref), (kc_ref, vc_ref), (kn_ref, vn_ref))):
        kbuf[n * blk:(n + 1) * blk, :] = kr[...]
        vbuf[n * blk:(n + 1) * blk, :] = vr[...]
    lane = lax.broadcasted_iota(I32, (1, LANES), 1)
    lo = lane < 64
    zero = jnp.zeros((), BF16)

    def scores(qi):
        r = rb * NA_QROWS + qi
        r_start = jnp.clip(r - NA_WIN_ROWS // 2, 0, rows - NA_WIN_ROWS)
        roff0 = r_start - r + (NA_WIN_ROWS - 1)
        off = pl.multiple_of((r_start - rb * NA_QROWS + NA_QROWS) * GRID_W, GRID_W)
        kw = kbuf[pl.ds(off, NA_WIN_ROWS * GRID_W), :]
        qp = q_ref[qi * GRID_W:(qi + 1) * GRID_W, :]
        q2 = jnp.concatenate([jnp.where(lo, qp, zero), jnp.where(lo, zero, qp)], axis=0)
        return lax.dot_general(q2, kw, NT, preferred_element_type=F32) + bt_ref[0, roff0], off

    nxt = scores(0)
    for qi in range(NA_QROWS):
        s, off = nxt
        if qi + 1 < NA_QROWS:
            nxt = scores(qi + 1)
        p = jnp.exp2(s - jnp.max(s, axis=-1, keepdims=True))
        l = jnp.sum(p, axis=-1, keepdims=True)
        o = jnp.dot(p.astype(BF16), vbuf[pl.ds(off, NA_WIN_ROWS * GRID_W), :], preferred_element_type=F32) / l
        o_ref[qi * GRID_W:(qi + 1) * GRID_W, :] = jnp.where(lo, o[:GRID_W], o[GRID_W:]).astype(o_ref.dtype)


def _na_bias_table(rel_bias):
    c = jnp.arange(GRID_W)
    c_start = jnp.clip(c - NA_WIN_COLS // 2, 0, GRID_W - NA_WIN_COLS)
    kc = jnp.arange(GRID_W)
    valid = (kc[None, :] >= c_start[:, None]) & (kc[None, :] < c_start[:, None] + NA_WIN_COLS)
    coff = jnp.clip(kc[None, :] - c[:, None] + (NA_WIN_COLS - 1), 0, 2 * NA_WIN_COLS - 2)
    onehot = jnp.logical_and(coff[None] == jnp.arange(2 * NA_WIN_COLS - 1)[:, None, None], valid[None])
    bm = jnp.einsum("hrj,jck->hrck", rel_bias * LOG2E, onehot.astype(F32), precision=lax.Precision.HIGHEST)
    bm = bm + jnp.where(valid, 0.0, NEG)[None, None]
    bt = jnp.stack([jnp.concatenate([bm[:, r0 + i] for i in range(NA_WIN_ROWS)], axis=-1)
                    for r0 in range(NA_WIN_ROWS)], axis=1).astype(F32)
    bt = bt.reshape(NA_HEADS // 2, 2, NA_WIN_ROWS, GRID_W, NA_WIN_ROWS * GRID_W)
    return jnp.swapaxes(bt, 1, 2).reshape(NA_HEADS // 2, NA_WIN_ROWS, 2 * GRID_W, NA_WIN_ROWS * GRID_W)


def _na_attention(qk, v, bt, bsz, seq_len):
    rows = seq_len // GRID_W
    assert rows % NA_QROWS == 0 and rows >= NA_WIN_ROWS
    nrb = rows // NA_QROWS
    blk = NA_QROWS * GRID_W
    npair = NA_HEADS // 2

    def spec(col0, shift):
        return pl.BlockSpec((blk, LANES),
                            lambda hp, b, rb: (b * nrb + jnp.clip(rb + shift, 0, nrb - 1), col0 + hp))
    in_specs = [spec(0, 0), spec(npair, -1), spec(npair, 0), spec(npair, 1),
                spec(0, -1), spec(0, 0), spec(0, 1),
                pl.BlockSpec((1, NA_WIN_ROWS, 2 * GRID_W, NA_WIN_ROWS * GRID_W), lambda hp, b, rb: (hp, 0, 0, 0))]
    return pl.pallas_call(
        functools.partial(_na_body, rows=rows), grid=(npair, bsz, nrb), in_specs=in_specs,
        out_specs=spec(0, 0),
        out_shape=jax.ShapeDtypeStruct((bsz * seq_len, D_MODEL), BF16),
        scratch_shapes=[pltpu.VMEM((3 * blk, LANES), BF16), pltpu.VMEM((3 * blk, LANES), BF16)],
        compiler_params=_cparams(3), name="na_attention")(qk, qk, qk, qk, v, v, v, bt)


HG_CHUNK = 128
HG_BLOCK = 512
HG_PAIR = 4


def _bcast_row(x, group, r):
    C = x.shape[0]
    x3 = x.reshape(C // group, group, x.shape[1])
    return jnp.broadcast_to(x3[:, r:r + 1, :], x3.shape).reshape(x.shape)


def _hgrn_body(xq_ref, xi_ref, xf_ref, lb_ref, o_ref, st_ref, *, rev, nchunk):
    C = HG_CHUNK

    @pl.when(pl.program_id(2) == 0)
    def _():
        st_ref[...] = jnp.zeros_like(st_ref)

    row = lax.broadcasted_iota(I32, (C, C), 0)
    col = lax.broadcasted_iota(I32, (C, C), 1)
    tri = jnp.where((row <= col) if rev else (row >= col), 1.0, 0.0).astype(BF16)
    t_idx = lax.broadcasted_iota(I32, (C, HG_DIM), 0)
    sub = t_idx % SUBLANES

    def one_head(rws, hh):
        cols = slice(hh * HG_DIM, (hh + 1) * HG_DIM)
        lbv = lb_ref[:, cols]
        q = xq_ref[rws, cols] * (HG_DIM ** -0.5)
        xi = xi_ref[rws, cols]
        v = xi * _sigmoid(xi)
        f = lbv + (1.0 - lbv) * _sigmoid(xf_ref[rws, cols])
        k = 1.0 - f
        lf = jnp.log(f) * LOG2E
        lf_hi = lf.astype(BF16)
        lf_lo = (lf - lf_hi.astype(F32)).astype(BF16)
        b = (jnp.dot(tri, lf_hi, preferred_element_type=F32)
             + jnp.dot(tri, lf_lo, preferred_element_type=F32))
        vb = v.astype(BF16)
        state = st_ref[hh]
        o = jnp.dot((q * jnp.exp2(b)).astype(BF16), state.astype(BF16), preferred_element_type=F32)
        for s in range(SUBLANES):
            msk = (sub <= s) if rev else (sub >= s)
            e = jnp.exp2(jnp.where(msk, b - _bcast_row(b, SUBLANES, s), NEG))
            sc = jnp.sum(q * _bcast_row(k, SUBLANES, s) * e, axis=-1, keepdims=True)
            o = o + sc * _bcast_row(v, SUBLANES, s)
        scores = jnp.zeros((C, C), F32)
        m = SUBLANES
        while m < C:
            right = ((t_idx // m) % 2) == 1
            rr = _bcast_row(b, 2 * m, m if rev else m - 1)
            qside = jnp.logical_not(right) if rev else right
            kside = right if rev else jnp.logical_not(right)
            qe = jnp.where(qside, q * jnp.exp2(jnp.where(qside, b - rr, 0.0)), 0.0)
            ke = jnp.where(kside, k * jnp.exp2(jnp.where(kside, rr - b, 0.0)), 0.0)
            sl = lax.dot_general(qe.astype(BF16), ke.astype(BF16), NT, preferred_element_type=F32)
            scores = scores + jnp.where((row // (2 * m)) == (col // (2 * m)), sl, 0.0)
            m *= 2
        o = o + jnp.dot(scores.astype(BF16), vb, preferred_element_type=F32)
        o_ref[rws, cols] = o
        bl = b[0:1, :] if rev else b[C - 1:C, :]
        ke = (k * jnp.exp2(bl - b)).astype(BF16)
        upd = lax.dot_general(ke, vb, TN, preferred_element_type=F32)
        decay = jnp.transpose(jnp.broadcast_to(jnp.exp2(bl), (HG_DIM, HG_DIM)))
        st_ref[hh] = state * decay + upd

    def chunk(ci, carry):
        cc = (nchunk - 1 - ci) if rev else ci
        rws = pl.ds(pl.multiple_of(cc * C, C), C)
        for hh in range(HG_PAIR):
            one_head(rws, hh)
        return carry

    lax.fori_loop(0, nchunk, chunk, 0)


def _hgrn_direction(y, lb, bsz, seq_len, rev):
    lbk = min(HG_BLOCK, seq_len)
    assert seq_len % lbk == 0 and lbk % HG_CHUNK == 0
    nb = seq_len // lbk
    fpart = 3 if rev else 2
    npair = HG_HEADS // HG_PAIR
    width = HG_PAIR * HG_DIM

    def spec(part):
        return pl.BlockSpec((lbk, width),
                            lambda b, h, c: (b * nb + ((nb - 1 - c) if rev else c), part * npair + h))
    return pl.pallas_call(
        functools.partial(_hgrn_body, rev=rev, nchunk=lbk // HG_CHUNK),
        grid=(bsz, npair, nb),
        in_specs=[spec(0), spec(1), spec(fpart), pl.BlockSpec((1, width), lambda b, h, c: (0, h))],
        out_specs=spec(0),
        out_shape=jax.ShapeDtypeStruct((bsz * seq_len, D_MODEL), F32),
        scratch_shapes=[pltpu.VMEM((HG_PAIR, HG_DIM, HG_DIM), F32)],
        compiler_params=_cparams(3), name="hgrn_bwd" if rev else "hgrn_fwd")(y, y, y, lb.reshape(1, D_MODEL))


def _flash_body(q_ref, k_ref, v_ref, o_ref, m_sc, acc_sc):
    ki = pl.program_id(3)

    @pl.when(ki == 0)
    def _():
        m_sc[...] = jnp.full_like(m_sc, NEG)
        acc_sc[...] = jnp.zeros_like(acc_sc)

    lane = lax.broadcasted_iota(I32, (1, LANES), 1)
    lo = lane < 64
    zero = jnp.zeros((), BF16)
    k = k_ref[...]
    v1 = jnp.where(lo, v_ref[...], jnp.ones((), BF16))

    def scores(hd):
        qp = q_ref[:, (hd // 2) * LANES:(hd // 2 + 1) * LANES]
        qm = jnp.where(lo if hd % 2 == 0 else jnp.logical_not(lo), qp, zero)
        return lax.dot_general(qm, k, NT, preferred_element_type=F32)

    n_heads = GQA_HEADS // GQA_KV
    s_next = scores(0)
    for hd in range(n_heads):
        s = s_next
        if hd + 1 < n_heads:
            s_next = scores(hd + 1)
        m_prev = m_sc[hd]
        m_new = jnp.maximum(m_prev, jnp.max(s, axis=-1, keepdims=True))
        p = jnp.exp2(s - m_new)
        acc_sc[hd] = (jnp.exp2(m_prev - m_new) * acc_sc[hd]
                      + jnp.dot(p.astype(BF16), v1, preferred_element_type=F32))
        m_sc[hd] = m_new

    @pl.when(ki == pl.num_programs(3) - 1)
    def _():
        for pair in range(2):
            a0 = acc_sc[2 * pair]
            a1 = pltpu.roll(acc_sc[2 * pair + 1], 64, 1)
            o_ref[:, pair * LANES:(pair + 1) * LANES] = jnp.where(
                lo, a0 / a0[:, 64:65], a1 / a1[:, 0:1]).astype(o_ref.dtype)


def _flash_gqa(qk, v, bsz, seq_len):
    tq = min(256, seq_len)
    tk = min(8192, seq_len)
    nq, nk = seq_len // tq, seq_len // tk
    gw = (GQA_HEADS // GQA_KV) * GQA_DH
    return pl.pallas_call(
        _flash_body, grid=(bsz, GQA_KV, nq, nk),
        in_specs=[pl.BlockSpec((tq, gw), lambda b, g, i, j: (b * nq + i, g)),
                  pl.BlockSpec((tk, LANES), lambda b, g, i, j: (b * nk + j, D_MODEL // LANES + g)),
                  pl.BlockSpec((tk, LANES), lambda b, g, i, j: (b * nk + j, g))],
        out_specs=pl.BlockSpec((tq, gw), lambda b, g, i, j: (b * nq + i, g)),
        out_shape=jax.ShapeDtypeStruct((bsz * seq_len, D_MODEL), BF16),
        scratch_shapes=[pltpu.VMEM((4, tq, 1), F32), pltpu.VMEM((4, tq, LANES), F32)],
        compiler_params=_cparams(4), name="flash_gqa")(qk, qk, v)


RET_CHUNK = 128
RET_BLOCK = 512


def _ret_body(q_ref, k_ref, v_ref, dm_ref, qd_ref, kd_ref, cd_ref, o_ref, st_ref, *, rev, nchunk):
    C = RET_CHUNK

    @pl.when(pl.program_id(2) == 0)
    def _():
        st_ref[...] = jnp.zeros_like(st_ref)

    def chunk(ci, carry):
        cc = (nchunk - 1 - ci) if rev else ci
        rws = pl.ds(pl.multiple_of(cc * C, C), C)
        q = q_ref[rws, :]
        k = k_ref[rws, :]
        v = v_ref[rws, :]
        state = st_ref[...]
        s = lax.dot_general(q, k, NT, preferred_element_type=F32) * dm_ref[0]
        o = jnp.dot(s.astype(BF16), v, preferred_element_type=F32)
        qs = (q.astype(F32) * qd_ref[0]).astype(BF16)
        o = o + jnp.dot(qs, state.astype(BF16), preferred_element_type=F32)
        o_ref[rws, :] = o
        ks = (k.astype(F32) * kd_ref[0]).astype(BF16)
        st_ref[...] = state * cd_ref[0] + lax.dot_general(ks, v, TN, preferred_element_type=F32)
        return carry

    lax.fori_loop(0, nchunk, chunk, 0)


def _ret_tables(rev):
    C = RET_CHUNK
    log_gamma = jnp.log1p(-jnp.exp2(-5.0 - jnp.arange(RET_HEADS, dtype=F32)))[:, None, None]
    pos = jnp.arange(C, dtype=F32)
    rel = pos[:, None] - pos[None, :]
    if rev:
        rel = -rel
        qpow, kpow = C - pos, pos
    else:
        qpow, kpow = pos + 1.0, C - 1.0 - pos
    dm = jnp.where(rel[None] >= 0, jnp.exp(rel[None] * log_gamma), 0.0)
    qd = jnp.broadcast_to(jnp.exp(qpow[None, :, None] * log_gamma), (RET_HEADS, C, RET_DK))
    kd = jnp.broadcast_to(jnp.exp(kpow[None, :, None] * log_gamma), (RET_HEADS, C, RET_DK))
    cd = jnp.broadcast_to(jnp.exp(C * log_gamma), (RET_HEADS, 1, RET_DV))
    return dm.astype(F32), qd.astype(F32), kd.astype(F32), cd.astype(F32)


def _ret_direction(qk, v, bsz, seq_len, rev):
    lbk = min(RET_BLOCK, seq_len)
    nb = seq_len // lbk
    rowblk = lambda b, h, c: b * nb + ((nb - 1 - c) if rev else c)
    tab = lambda shape: pl.BlockSpec((1,) + shape, lambda b, h, c: (h, 0, 0))
    return pl.pallas_call(
        functools.partial(_ret_body, rev=rev, nchunk=lbk // RET_CHUNK),
        grid=(bsz, RET_HEADS, nb),
        in_specs=[pl.BlockSpec((lbk, RET_DK), lambda b, h, c: (rowblk(b, h, c), h)),
                  pl.BlockSpec((lbk, RET_DK), lambda b, h, c: (rowblk(b, h, c), RET_HEADS + h)),
                  pl.BlockSpec((lbk, RET_DV), lambda b, h, c: (rowblk(b, h, c), h)),
                  tab((RET_CHUNK, RET_CHUNK)), tab((RET_CHUNK, RET_DK)), tab((RET_CHUNK, RET_DK)),
                  tab((1, RET_DV))],
        out_specs=pl.BlockSpec((lbk, RET_DV), lambda b, h, c: (rowblk(b, h, c), h)),
        out_shape=jax.ShapeDtypeStruct((bsz * seq_len, RET_HEADS * RET_DV), F32),
        scratch_shapes=[pltpu.VMEM((RET_DK, RET_DV), F32)],
        compiler_params=_cparams(3), name="ret_bwd" if rev else "ret_fwd")(qk, qk, v, *_ret_tables(rev))


MOE_FF_CHUNK = 512


MOE_SLOTS = 3


def _moe_body(idx_ref, nx1_ref, nx2_ref, u_hbm, gate_ref, wg_ref, wu_ref, wd_ref, hi_ref, lo_ref, xbuf, sem, *, tc):
    nsteps = pl.num_programs(0) * pl.num_programs(1)
    step = pl.program_id(0) * pl.num_programs(1) + pl.program_id(1)
    slot = step % MOE_SLOTS

    def row_copy(token, r, s):
        src = u_hbm.at[pl.ds(pl.multiple_of(token * SUBLANES, SUBLANES), SUBLANES)]
        return pltpu.make_async_copy(src, xbuf.at[s, pl.ds(r * SUBLANES, SUBLANES)], sem.at[s])

    def wait_rows(s):
        for _ in range(tc):
            row_copy(0, 0, s).wait()

    @pl.when(step == 0)
    def _():
        def issue(r, carry):
            row_copy(idx_ref[0, 0, r], r, 0).start()
            row_copy(nx1_ref[0, 0, r], r, 1).start()
            return carry
        lax.fori_loop(0, tc, issue, 0)

    wait_rows(slot)
    x = jnp.concatenate([xbuf[slot, pl.ds(s, tc, stride=SUBLANES), :] for s in range(SUBLANES)],
                        axis=1).astype(BF16)
    ahead = (step + 2) % MOE_SLOTS
    for r in range(tc):
        row_copy(nx2_ref[0, 0, r], r, ahead).start()
    acc = jnp.zeros((tc, D_MODEL), F32)
    for f in range(EXPERT_FF // MOE_FF_CHUNK):
        cols = slice(f * MOE_FF_CHUNK, (f + 1) * MOE_FF_CHUNK)
        g = jnp.dot(x, wg_ref[0, 0, :, cols], preferred_element_type=F32)
        up = jnp.dot(x, wu_ref[0, 0, :, cols], preferred_element_type=F32)
        hid = ((g * _sigmoid(g)) * up).astype(BF16)
        acc = acc + jnp.dot(hid, wd_ref[0, 0, cols, :], preferred_element_type=F32)
    gate = gate_ref[0]
    for c in range(D_MODEL // LANES):
        cols = slice(c * LANES, (c + 1) * LANES)
        ye = acc[:, cols] * gate
        hi = ye.astype(BF16)
        hi_ref[0, :, cols] = hi
        lo_ref[0, :, cols] = (ye - hi.astype(F32)).astype(BF16)

    @pl.when(step == nsteps - 1)
    def _():
        wait_rows((step + 1) % MOE_SLOTS)
        wait_rows(ahead)


def _moe_experts(u, idx_sorted, gate_sorted, wg, wu, wd, layer):
    cap = idx_sorted.shape[1]
    tc = min(512, cap)
    nj = cap // tc
    nsteps = N_EXPERTS * nj
    assert nsteps >= MOE_SLOTS
    idx3 = idx_sorted.reshape(nsteps, 1, tc)
    gate_b = jnp.broadcast_to(gate_sorted[:, :, None], (N_EXPERTS, cap, LANES))
    ispec = lambda d: pl.BlockSpec((1, 1, tc), lambda e, j: (jnp.minimum(e * nj + j + d, nsteps - 1), 0, 0),
                                   memory_space=pltpu.SMEM)
    wspec = lambda shape: pl.BlockSpec((1, 1) + shape, lambda e, j: (layer, e, 0, 0))
    ospec = pl.BlockSpec((1, tc, D_MODEL), lambda e, j: (e, j, 0))
    oshape = jax.ShapeDtypeStruct((N_EXPERTS, cap, D_MODEL), BF16)
    return pl.pallas_call(
        functools.partial(_moe_body, tc=tc), grid=(N_EXPERTS, nj),
        in_specs=[ispec(0), ispec(1), ispec(2),
                  pl.BlockSpec(memory_space=pl.ANY),
                  pl.BlockSpec((1, tc, LANES), lambda e, j: (e, j, 0)),
                  wspec((D_MODEL, EXPERT_FF)), wspec((D_MODEL, EXPERT_FF)), wspec((EXPERT_FF, D_MODEL))],
        out_specs=[ospec, ospec], out_shape=[oshape, oshape],
        scratch_shapes=[pltpu.VMEM((MOE_SLOTS, tc * SUBLANES, LANES), F32),
                        pltpu.SemaphoreType.DMA((MOE_SLOTS,))],
        compiler_params=_cparams(2), name="moe_experts")(idx3, idx3, idx3, u, gate_b, wg, wu, wd)


CMB_TOKENS = 256
CMB_WINDOW = LANES // 2
CMB_ALIGN = 2 * SUBLANES


def _combine_body(base_ref, nr_ref, h_ref, pos_ref, hi_hbm, lo_hbm, o_ref, buf, xbuf, sem, xsem, *, cap, ntiles):
    i = pl.program_id(0)
    slot = i % 2
    Wn = CMB_WINDOW
    lane = lax.broadcasted_iota(I32, (1, LANES), 1)
    lo_half = lane < Wn

    def window(tile, e, k):
        start = (base_ref[e * ntiles + tile] // CMB_ALIGN) * CMB_ALIGN + k * Wn
        w0 = pl.multiple_of(jnp.minimum(start, cap - Wn), CMB_ALIGN)
        return start, w0

    def copies(tile, k, dst, dsem):
        out = []
        for e in range(N_EXPERTS):
            w0 = window(tile, e, k)[1]
            out.append(pltpu.make_async_copy(hi_hbm.at[e, pl.ds(w0, Wn)], dst.at[0, e], dsem))
            out.append(pltpu.make_async_copy(lo_hbm.at[e, pl.ds(w0, Wn)], dst.at[1, e], dsem))
        return out

    def one_hot(k, check_start):
        parts = []
        for e in range(0, N_EXPERTS, 2):
            s0, w0 = window(i, e, k)
            s1, w1 = window(i, e + 1, k)
            p0 = pos_ref[:, e:e + 1]
            p1 = pos_ref[:, e + 1:e + 2]
            hit = jnp.where(lo_half, p0 - w0, p1 - w1 + Wn) == lane
            if check_start:
                hit = jnp.logical_and(hit, jnp.where(lo_half, p0 - s0, p1 - s1) >= 0)
            parts.append(jnp.where(hit, 1.0, 0.0).astype(BF16))
        return jnp.concatenate(parts, axis=1)

    def placed(smat, src):
        return (jnp.dot(smat, src[0].reshape(N_EXPERTS * Wn, D_MODEL), preferred_element_type=F32)
                + jnp.dot(smat, src[1].reshape(N_EXPERTS * Wn, D_MODEL), preferred_element_type=F32))

    @pl.when(i == 0)
    def _():
        for c in copies(0, 0, buf.at[0], sem.at[0]):
            c.start()

    @pl.when(i + 1 < ntiles)
    def _():
        for c in copies(i + 1, 0, buf.at[1 - slot], sem.at[1 - slot]):
            c.start()

    smat = one_hot(0, False)
    for c in copies(i, 0, buf.at[slot], sem.at[slot]):
        c.wait()
    acc = h_ref[...] + placed(smat, buf[slot])

    def extra_round(k, acc):
        for c in copies(i, k, xbuf, xsem):
            c.start()
        smat = one_hot(k, True)
        for c in copies(i, k, xbuf, xsem):
            c.wait()
        return acc + placed(smat, xbuf[...])

    o_ref[...] = lax.fori_loop(1, nr_ref[i], extra_round, acc)


def _moe_combine(h, ye_hi, ye_lo, pos_t, base, nrounds):
    T = h.shape[0]
    cap = ye_hi.shape[1]
    ntiles = T // CMB_TOKENS
    assert cap >= CMB_WINDOW and cap % CMB_ALIGN == 0
    wshape = (2, N_EXPERTS, CMB_WINDOW, D_MODEL)
    grid_spec = pltpu.PrefetchScalarGridSpec(
        num_scalar_prefetch=2, grid=(ntiles,),
        in_specs=[pl.BlockSpec((CMB_TOKENS, D_MODEL), lambda i, b, n: (i, 0)),
                  pl.BlockSpec((CMB_TOKENS, N_EXPERTS), lambda i, b, n: (i, 0)),
                  pl.BlockSpec(memory_space=pl.ANY), pl.BlockSpec(memory_space=pl.ANY)],
        out_specs=pl.BlockSpec((CMB_TOKENS, D_MODEL), lambda i, b, n: (i, 0)),
        scratch_shapes=[pltpu.VMEM((2,) + wshape, BF16), pltpu.VMEM(wshape, BF16),
                        pltpu.SemaphoreType.DMA((2,)), pltpu.SemaphoreType.DMA(())])
    return pl.pallas_call(
        functools.partial(_combine_body, cap=cap, ntiles=ntiles), grid_spec=grid_spec,
        out_shape=jax.ShapeDtypeStruct((T, D_MODEL), F32),
        compiler_params=_cparams(1), name="moe_combine")(base.reshape(-1), nrounds, h, pos_t, ye_hi, ye_lo)


SEL_ROWS = 2


def _flag(cond):
    return jnp.where(cond, 1.0, 0.0)


def _select_body(aff_ref, pos_ref, idx_ref, gate_ref, thr_sc, *, cap):
    n_exp, n_chunk, _ = aff_ref.shape

    def bit_step(i, prefix):
        cand = prefix | jnp.left_shift(jnp.int32(1), 30 - i)
        hit = _flag(pltpu.bitcast(aff_ref[...], I32) >= cand)
        cnt = jnp.sum(jnp.sum(hit, axis=1, keepdims=True), axis=2, keepdims=True)
        return jnp.where(cnt >= cap, cand, prefix)

    thr = lax.fori_loop(0, 31, bit_step, jnp.zeros((n_exp, 1, 1), I32))
    thr_sc[...] = jnp.broadcast_to(thr, thr_sc.shape)

    li = lax.broadcasted_iota(I32, (LANES, LANES), 0)
    lj = lax.broadcasted_iota(I32, (LANES, LANES), 1)
    upper = _flag(li <= lj).astype(BF16)
    ci = lax.broadcasted_iota(I32, (n_chunk, n_chunk), 0)
    cj = lax.broadcasted_iota(I32, (n_chunk, n_chunk), 1)
    before = _flag(cj < ci).astype(BF16)
    chunk_id = lax.broadcasted_iota(I32, (n_chunk, LANES), 0).astype(F32)
    lane_id = lax.broadcasted_iota(I32, (LANES, LANES), 0).astype(F32)
    slot_lane = lax.broadcasted_iota(I32, (1, LANES), 1)

    def counts(flags):
        local = jnp.dot(flags.astype(BF16), upper, preferred_element_type=F32)
        total = jnp.broadcast_to(local[:, LANES - 1:LANES], local.shape)
        return local, total, jnp.dot(before, total.astype(BF16), preferred_element_type=F32)

    def per_expert(e, carry):
        aff = aff_ref[e]
        key = pltpu.bitcast(aff, I32)
        t = thr_sc[e]
        gt = _flag(key > t)
        eq = _flag(key == t)
        need = cap - jnp.sum(jnp.sum(gt, axis=1, keepdims=True), axis=0, keepdims=True)
        eq_local, _, eq_off = counts(eq)
        sel = gt + eq * _flag(eq_local + eq_off - eq < need)
        local, total, off = counts(sel)
        pos_ref[e] = jnp.where(sel > 0.0, local + off - 1.0, -1.0).astype(I32)
        reached = off + total
        local_t = jnp.transpose(local).astype(BF16)
        aff_tr = jnp.transpose(aff)

        def one_row(r):
            slot = (r * LANES + slot_lane).astype(F32)
            chunk = jnp.sum(_flag(reached <= slot), axis=0, keepdims=True)
            skipped = jnp.sum(jnp.where(chunk_id < chunk, total, 0.0), axis=0, keepdims=True)
            pick = _flag(chunk_id == chunk)
            run = jnp.dot(local_t, pick.astype(BF16), preferred_element_type=F32)
            lane = jnp.sum(_flag(run <= slot - skipped), axis=0, keepdims=True)
            idx_ref[e, pl.ds(r, 1), :] = (chunk * LANES + lane).astype(I32)
            vals = jnp.dot(aff_tr, pick, precision=lax.Precision.HIGHEST, preferred_element_type=F32)
            gate_ref[e, pl.ds(r, 1), :] = jnp.sum(jnp.where(lane_id == lane, vals, 0.0), axis=0, keepdims=True)

        def row_group(g, c2):
            for i in range(SEL_ROWS):
                one_row(g * SEL_ROWS + i)
            return c2

        lax.fori_loop(0, cap // (LANES * SEL_ROWS), row_group, 0)
        return carry

    lax.fori_loop(0, n_exp, per_expert, 0)


def _select(aff_t, cap):
    E, T = aff_t.shape
    assert T % LANES == 0 and cap % (LANES * SEL_ROWS) == 0
    full = lambda shape: pl.BlockSpec(shape, lambda i: (0, 0, 0))
    shapes = [(E, T // LANES, LANES), (E, cap // LANES, LANES), (E, cap // LANES, LANES)]
    pos, idx, gate = pl.pallas_call(
        functools.partial(_select_body, cap=cap), grid=(1,),
        in_specs=[full(shapes[0])], out_specs=[full(s) for s in shapes],
        out_shape=[jax.ShapeDtypeStruct(shapes[0], I32), jax.ShapeDtypeStruct(shapes[1], I32),
                   jax.ShapeDtypeStruct(shapes[2], F32)],
        scratch_shapes=[pltpu.VMEM((E, 1, LANES), I32)],
        compiler_params=_cparams(1), name="moe_select")(aff_t.reshape(shapes[0]))
    return pos.reshape(E, T), idx.reshape(E, cap), gate.reshape(E, cap)


def _route(aff_t, cap):
    E, T = aff_t.shape
    ntiles = T // CMB_TOKENS
    pos, idx_s, gate_s = _select(aff_t, cap)
    cnt = jnp.sum((pos >= 0).reshape(E, ntiles, CMB_TOKENS), axis=-1, dtype=I32)
    base = jnp.cumsum(cnt, axis=1, dtype=I32) - cnt
    need = jnp.where(cnt > 0, (base % CMB_ALIGN + cnt + CMB_WINDOW - 1) // CMB_WINDOW, 0)
    return idx_s, gate_s, pos.T, base, jnp.max(need, axis=0).astype(I32)


def _moe_layer(h, u, aff_t, wg, wu, wd, layer):
    T = h.shape[0]
    cap = CAPACITY_FACTOR * T // N_EXPERTS
    idx_s, gate_s, pos_t, base, nrounds = _route(aff_t, cap)
    ye_hi, ye_lo = _moe_experts(u, idx_s, gate_s, wg, wu, wd, layer)
    return _moe_combine(h, ye_hi, ye_lo, pos_t, base, nrounds)


def _rope_tables(seq_len, n_freq, reps):
    t = jnp.arange(seq_len)
    inv_freq = ROPE_THETA ** (-jnp.arange(n_freq, dtype=F32) / n_freq)
    out = []
    for pos in ((t // GRID_W).astype(F32), (t % GRID_W).astype(F32)):
        ang = pos[:, None] * inv_freq[None, :]
        c = jnp.concatenate([jnp.cos(ang), jnp.cos(ang)], axis=1)
        s = jnp.concatenate([-jnp.sin(ang), jnp.sin(ang)], axis=1)
        out.append((c, s))
    cos = jnp.concatenate([out[0][0], out[1][0]], axis=1)
    sin = jnp.concatenate([out[0][1], out[1][1]], axis=1)
    return jnp.tile(cos, (1, reps)), jnp.tile(sin, (1, reps))


def _dup_heads(w, n_heads, dh):
    k = w.shape[0]
    return jnp.repeat(w.reshape(k, n_heads, 1, dh), 2, axis=2).reshape(k, n_heads * 2 * dh)


def _trunk(x, p):
    bsz, seq_len, _ = x.shape
    T = bsz * seq_len
    h = x.reshape(T, D_MODEL)
    for layer in range(4):
        nm = p["norm_mix"][layer]
        if layer == 0:
            qk = _proj(h, nm, p["na_wqk"], seq_len, BF16, headnorm=True, colvec=p["na_qk_gain"], name="na_proj_qk")
            v = _proj(h, nm, p["na_wv"], seq_len, BF16, name="na_proj_v")
            mix_args = (_na_attention(qk, v, p["na_bt"], bsz, seq_len),)
            kind, w_out = "plain", p["na_wo"]
        elif layer == 1:
            y = _proj(h, nm, p["hg_win"], seq_len, F32, name="hg_proj")
            o_f = _hgrn_direction(y, p["hg_lb"], bsz, seq_len, False)
            o_b = _hgrn_direction(y, p["hg_lb"], bsz, seq_len, True)
            mix_args = (o_f, o_b, y, 4, p["hg_o_gain"])
            kind, w_out = "hg", p["hg_wo"]
        elif layer == 2:
            cos, sin = _rope_tables(seq_len, 16, 2)
            qk = _proj(h, nm, p["gq_wqk"], seq_len, BF16, headnorm=True, colvec=p["gq_qk_gain"],
                       rope=(16, cos, sin), name="gqa_proj_qk")
            v = _proj(h, nm, p["gq_wv"], seq_len, BF16, name="gqa_proj_v")
            mix_args = (_flash_gqa(qk, v, bsz, seq_len),)
            kind, w_out = "plain", p["gq_wo"]
        else:
            cos, sin = _rope_tables(seq_len, 64, 1)
            qk = _proj(h, nm, p["rt_wqk"], seq_len, BF16, colvec=p["rt_qk_scale"],
                       rope=(64, cos, sin), name="ret_proj_qk")
            v = _proj(h, nm, p["rt_wv"], seq_len, BF16, name="ret_proj_v")
            g = _proj(h, nm, p["rt_wg"], seq_len, F32, name="ret_proj_g")
            o_f = _ret_direction(qk, v, bsz, seq_len, False)
            o_b = _ret_direction(qk, v, bsz, seq_len, True)
            mix_args = (o_f, o_b, g, 0, p["rt_o_gain"])
            kind, w_out = "rt", p["rt_wo"]
        h, u, aff_t = _outproj(h, mix_args, w_out, p["norm_ffn"][layer], p["router_t"][layer], kind,
                               name=f"outproj_{kind}")
        h = _moe_layer(h, u, aff_t, p["moe_wg"], p["moe_wu"], p["moe_wd"], layer)
    return h.reshape(bsz, seq_len, D_MODEL)


def kernel(x_prompt, x_sample, norm_mix, norm_ffn, na_w_in, na_q_gain, na_k_gain, na_rel_bias, na_w_out, hg_w_in, hg_lb, hg_o_gain, hg_w_out, gq_w_in, gq_q_gain, gq_k_gain, gq_w_out, rt_w_in, rt_o_gain, rt_w_out, moe_router, moe_w_gate, moe_w_up, moe_w_down):
    bf = lambda a: a.astype(BF16)
    na_w, gq_w, rt_w = na_w_in[0], gq_w_in[0], rt_w_in[0]
    lb_cum = jnp.cumsum(jax.nn.softmax(hg_lb.astype(F32), axis=0), axis=0)
    qd = GQA_HEADS * GQA_DH
    kd = GQA_KV * GQA_DH
    rq = RET_HEADS * RET_DK
    rv = RET_HEADS * RET_DV
    p = {
        "norm_mix": norm_mix, "norm_ffn": norm_ffn,
        "na_wqk": bf(na_w[:, :2 * D_MODEL]), "na_wv": bf(na_w[:, 2 * D_MODEL:]),
        "na_qk_gain": jnp.concatenate([jnp.tile(na_q_gain[0], NA_HEADS) * (64 ** -0.5 * LOG2E),
                                       jnp.tile(na_k_gain[0], NA_HEADS)]),
        "na_bt": _na_bias_table(na_rel_bias[0]), "na_wo": bf(na_w_out[0]),
        "hg_win": bf(hg_w_in[0]), "hg_lb": lb_cum[1] - lb_cum[0], "hg_o_gain": hg_o_gain[0],
        "hg_wo": bf(hg_w_out[0]),
        "gq_wqk": bf(jnp.concatenate([gq_w[:, :qd], _dup_heads(gq_w[:, qd:qd + kd], GQA_KV, GQA_DH)], axis=1)),
        "gq_wv": bf(_dup_heads(gq_w[:, qd + kd:], GQA_KV, GQA_DH)),
        "gq_qk_gain": jnp.concatenate([jnp.tile(gq_q_gain[0], GQA_HEADS) * (GQA_DH ** -0.5 * LOG2E),
                                       jnp.tile(gq_k_gain[0], 2 * GQA_KV)]),
        "gq_wo": bf(gq_w_out[0]),
        "rt_wqk": bf(rt_w[:, :2 * rq]), "rt_wv": bf(rt_w[:, 2 * rq:2 * rq + rv]), "rt_wg": bf(rt_w[:, 2 * rq + rv:]),
        "rt_qk_scale": jnp.concatenate([jnp.full((rq,), RET_DK ** -0.5, F32), jnp.ones((rq,), F32)]),
        "rt_o_gain": rt_o_gain[0], "rt_wo": bf(rt_w_out[0]),
        "router_t": jnp.swapaxes(moe_router, 1, 2).astype(F32),
        "moe_wg": bf(moe_w_gate), "moe_wu": bf(moe_w_up), "moe_wd": bf(moe_w_down),
    }
    return (_trunk(x_prompt, p), _trunk(x_sample, p))
```

```python
import functools
import math

import jax
import jax.numpy as jnp
from jax import lax
from jax.experimental import pallas as pl
from jax.experimental.pallas import tpu as pltpu

F32 = jnp.float32
BF16 = jnp.bfloat16
I32 = jnp.int32

D_MODEL = 1024
GRID_W = 64
EPS = 1e-6
ROPE_THETA = 10000.0
NEG = -1e30
LOG2E = math.log2(math.e)

NA_HEADS = 16
NA_WIN_ROWS = 8
NA_WIN_COLS = 16
HG_HEADS = 8
HG_DIM = 128
GQA_HEADS = 16
GQA_KV = 4
GQA_DH = 64
RET_HEADS = 4
RET_DK = 256
RET_DV = 512
N_EXPERTS = 16
EXPERT_FF = 2048
CAPACITY_FACTOR = 2

LANES = 128
SUBLANES = 8
VMEM_LIMIT = 52 * 1024 * 1024

NT = (((1,), (1,)), ((), ()))
TN = (((0,), (0,)), ((), ()))


def _cparams(n_axes):
    return pltpu.CompilerParams(dimension_semantics=("arbitrary",) * n_axes,
                                vmem_limit_bytes=VMEM_LIMIT)


def _sigmoid(x):
    return 1.0 / (1.0 + jnp.exp(-x))


def _proj_body(*refs, headnorm, has_cv, rope_k, tw):
    it = iter(refs)
    x_ref, g_ref, w_ref = next(it), next(it), next(it)
    cv_ref = next(it) if has_cv else None
    cos_ref = next(it) if rope_k else None
    sin_ref = next(it) if rope_k else None
    o_ref, xn_ref = next(it), next(it)

    @pl.when(pl.program_id(1) == 0)
    def _():
        x = x_ref[...]
        ms = jnp.mean(x * x, axis=-1, keepdims=True)
        xn_ref[...] = (x * lax.rsqrt(ms + EPS) * g_ref[...]).astype(BF16)

    acc = jnp.dot(xn_ref[...], w_ref[pl.program_id(1)], preferred_element_type=F32)
    if not (headnorm or has_cv or rope_k):
        o_ref[...] = acc.astype(o_ref.dtype)
        return
    lane = lax.broadcasted_iota(I32, (1, LANES), 1)
    if headnorm:
        same_head = _flag((lax.broadcasted_iota(I32, (LANES, LANES), 0) < 64)
                          == (lax.broadcasted_iota(I32, (LANES, LANES), 1) < 64)).astype(BF16)
    for c in range(acc.shape[1] // LANES):
        cols = slice(c * LANES, (c + 1) * LANES)
        a = acc[:, cols]
        if headnorm:
            sq = a * a
            sq_hi = sq.astype(BF16)
            sq_lo = (sq - sq_hi.astype(F32)).astype(BF16)
            ssq = (jnp.dot(sq_hi, same_head, preferred_element_type=F32)
                   + jnp.dot(sq_lo, same_head, preferred_element_type=F32))
            a = a * lax.rsqrt(ssq * (1.0 / 64.0) + EPS)
        if has_cv:
            a = a * cv_ref[:, cols]
        if rope_k:
            tcol = (c * LANES) % tw
            if rope_k == 64:
                partner = pltpu.roll(a, 64, 1)
            else:
                partner = jnp.where((lane & rope_k) != 0, pltpu.roll(a, rope_k, 1),
                                    pltpu.roll(a, LANES - rope_k, 1))
            a = a * cos_ref[:, tcol:tcol + LANES] + partner * sin_ref[:, tcol:tcol + LANES]
        o_ref[:, cols] = a.astype(o_ref.dtype)


def _proj(x, g, w, seq_len, out_dtype, *, headnorm=False, colvec=None, rope=None, name="proj"):
    T = x.shape[0]
    N = w.shape[1]
    tm = min(1024, seq_len)
    tn = 512 if N % 512 == 0 else 256
    assert T % tm == 0 and seq_len % tm == 0 and N % tn == 0
    nsb = seq_len // tm
    in_specs = [pl.BlockSpec((tm, D_MODEL), lambda i, j: (i, 0)),
                pl.BlockSpec((1, D_MODEL), lambda i, j: (0, 0)),
                pl.BlockSpec((N // tn, D_MODEL, tn), lambda i, j: (0, 0, 0))]
    w_tiles = jnp.swapaxes(w.reshape(D_MODEL, N // tn, tn), 0, 1)
    args = [x, g.reshape(1, D_MODEL).astype(F32), w_tiles]
    if colvec is not None:
        in_specs.append(pl.BlockSpec((1, tn), lambda i, j: (0, j)))
        args.append(colvec.reshape(1, N).astype(F32))
    rope_k, tw = 0, LANES
    if rope is not None:
        rope_k, cos, sin = rope
        tw = cos.shape[1]
        assert tn % tw == 0
        in_specs += [pl.BlockSpec((tm, tw), lambda i, j: (i % nsb, 0))] * 2
        args += [cos, sin]
    body = functools.partial(_proj_body, headnorm=headnorm, has_cv=colvec is not None,
                             rope_k=rope_k, tw=tw)
    return pl.pallas_call(
        body, grid=(T // tm, N // tn), in_specs=in_specs,
        out_specs=pl.BlockSpec((tm, tn), lambda i, j: (i, j)),
        out_shape=jax.ShapeDtypeStruct((T, N), out_dtype),
        scratch_shapes=[pltpu.VMEM((tm, D_MODEL), BF16)],
        compiler_params=_cparams(2), name=name)(*args)


def _group_rms(o, gain_ref, width):
    parts = []
    for c in range(o.shape[1] // width):
        a = o[:, c * width:(c + 1) * width]
        ms = jnp.mean(a * a, axis=-1, keepdims=True)
        parts.append(a * lax.rsqrt(ms + EPS) * gain_ref[...])
    return jnp.concatenate(parts, axis=1)


OUTPROJ_SPLIT = 2


def _outproj_body(*refs, kind):
    it = iter(refs)
    h_ref = next(it)
    if kind == "plain":
        a_ref = next(it)
    else:
        of_ref, ob_ref, g_ref, gain_ref = next(it), next(it), next(it), next(it)
    w_ref, nf_ref, wr_ref = next(it), next(it), next(it)
    hn_ref, u_ref, aff_ref = next(it), next(it), next(it)
    hm = h_ref.shape[0] // OUTPROJ_SPLIT
    for part in range(OUTPROJ_SPLIT):
        rows = slice(part * hm, (part + 1) * hm)
        if kind == "plain":
            a = a_ref[rows, :]
        else:
            o = of_ref[rows, :] + ob_ref[rows, :]
            g = g_ref[rows, :]
            if kind == "hg":
                a = (_group_rms(o, gain_ref, HG_DIM) * _sigmoid(g)).astype(BF16)
            else:
                a = ((g * _sigmoid(g)) * _group_rms(o, gain_ref, RET_DV)).astype(BF16)
        hn = h_ref[rows, :] + jnp.dot(a, w_ref[...], preferred_element_type=F32)
        hn_ref[rows, :] = hn
        ms = jnp.mean(hn * hn, axis=-1, keepdims=True)
        u = hn * lax.rsqrt(ms + EPS) * nf_ref[...]
        for s in range(SUBLANES):
            u_ref[pl.ds(part * hm * SUBLANES + s, hm, stride=SUBLANES), :] = u[:, s * LANES:(s + 1) * LANES]
        logits = lax.dot_general(wr_ref[...], u, NT, precision=lax.Precision.HIGHEST,
                                 preferred_element_type=F32)
        m = jnp.max(logits, axis=0, keepdims=True)
        e = jnp.exp(logits - m)
        aff_ref[:, rows] = e / jnp.sum(e, axis=0, keepdims=True)


def _outproj(h, mix_args, w_out, norm_g, w_router_t, kind, name):
    T = h.shape[0]
    tm = 256 if kind == "rt" else 512
    K = w_out.shape[0]
    row = lambda i: (i, 0)
    in_specs = [pl.BlockSpec((tm, D_MODEL), row)]
    args = [h]
    if kind == "plain":
        (a,) = mix_args
        in_specs.append(pl.BlockSpec((tm, K), row))
        args.append(a)
    else:
        o_f, o_b, g_arr, g_col, gain = mix_args
        gw = gain.shape[0]
        in_specs += [pl.BlockSpec((tm, K), row), pl.BlockSpec((tm, K), row),
                     pl.BlockSpec((tm, K), lambda i: (i, g_col)),
                     pl.BlockSpec((1, gw), lambda i: (0, 0))]
        args += [o_f, o_b, g_arr, gain.reshape(1, gw).astype(F32)]
    in_specs += [pl.BlockSpec((K, D_MODEL), lambda i: (0, 0)),
                 pl.BlockSpec((1, D_MODEL), lambda i: (0, 0)),
                 pl.BlockSpec((N_EXPERTS, D_MODEL), lambda i: (0, 0))]
    args += [w_out, norm_g.reshape(1, D_MODEL).astype(F32), w_router_t]
    return pl.pallas_call(
        functools.partial(_outproj_body, kind=kind), grid=(T // tm,), in_specs=in_specs,
        out_specs=[pl.BlockSpec((tm, D_MODEL), row), pl.BlockSpec((tm * SUBLANES, LANES), row),
                   pl.BlockSpec((N_EXPERTS, tm), lambda i: (0, i))],
        out_shape=[jax.ShapeDtypeStruct((T, D_MODEL), F32), jax.ShapeDtypeStruct((T * SUBLANES, LANES), F32),
                   jax.ShapeDtypeStruct((N_EXPERTS, T), F32)],
        compiler_params=_cparams(1), name=name)(*args)


NA_QROWS = 8


def _na_body(q_ref, kp_ref, kc_ref, kn_ref, vp_ref, vc_ref, vn_ref, bt_ref, o_ref, kbuf, vbuf, *, rows):
    rb = pl.program_id(2)
    blk = NA_QROWS * GRID_W
    for n, (kr, vr) in enumerate(((kp_ref, vp_ref), (kc_ref, vc_ref), (kn_ref, vn_ref))):
        kbuf[n * blk:(n + 1) * blk, :] = kr[...]
        vbuf[n * blk:(n + 1) * blk, :] = vr[...]
    lane = lax.broadcasted_iota(I32, (1, LANES), 1)
    lo = lane < 64
    zero = jnp.zeros((), BF16)

    def scores(qi):
        r = rb * NA_QROWS + qi
        r_start = jnp.clip(r - NA_WIN_ROWS // 2, 0, rows - NA_WIN_ROWS)
        roff0 = r_start - r + (NA_WIN_ROWS - 1)
        off = pl.multiple_of((r_start - rb * NA_QROWS + NA_QROWS) * GRID_W, GRID_W)
        kw = kbuf[pl.ds(off, NA_WIN_ROWS * GRID_W), :]
        qp = q_ref[qi * GRID_W:(qi + 1) * GRID_W, :]
        q2 = jnp.concatenate([jnp.where(lo, qp, zero), jnp.where(lo, zero, qp)], axis=0)
        return lax.dot_general(q2, kw, NT, preferred_element_type=F32) + bt_ref[0, roff0], off

    nxt = scores(0)
    for qi in range(NA_QROWS):
        s, off = nxt
        if qi + 1 < NA_QROWS:
            nxt = scores(qi + 1)
        p = jnp.exp2(s - jnp.max(s, axis=-1, keepdims=True))
        l = jnp.sum(p, axis=-1, keepdims=True)
        o = jnp.dot(p.astype(BF16), vbuf[pl.ds(off, NA_WIN_ROWS * GRID_W), :], preferred_element_type=F32) / l
        o_ref[qi * GRID_W:(qi + 1) * GRID_W, :] = jnp.where(lo, o[:GRID_W], o[GRID_W:]).astype(o_ref.dtype)


def _na_bias_table(rel_bias):
    c = jnp.arange(GRID_W)
    c_start = jnp.clip(c - NA_WIN_COLS // 2, 0, GRID_W - NA_WIN_COLS)
    kc = jnp.arange(GRID_W)
    valid = (kc[None, :] >= c_start[:, None]) & (kc[None, :] < c_start[:, None] + NA_WIN_COLS)
    coff = jnp.clip(kc[None, :] - c[:, None] + (NA_WIN_COLS - 1), 0, 2 * NA_WIN_COLS - 2)
    onehot = jnp.logical_and(coff[None] == jnp.arange(2 * NA_WIN_COLS - 1)[:, None, None], valid[None])
    bm = jnp.einsum("hrj,jck->hrck", rel_bias * LOG2E, onehot.astype(F32), precision=lax.Precision.HIGHEST)
    bm = bm + jnp.where(valid, 0.0, NEG)[None, None]
    bt = jnp.stack([jnp.concatenate([bm[:, r0 + i] for i in range(NA_WIN_ROWS)], axis=-1)
                    for r0 in range(NA_WIN_ROWS)], axis=1).astype(F32)
    bt = bt.reshape(NA_HEADS // 2, 2, NA_WIN_ROWS, GRID_W, NA_WIN_ROWS * GRID_W)
    return jnp.swapaxes(bt, 1, 2).reshape(NA_HEADS // 2, NA_WIN_ROWS, 2 * GRID_W, NA_WIN_ROWS * GRID_W)


def _na_attention(qk, v, bt, bsz, seq_len):
    rows = seq_len // GRID_W
    assert rows % NA_QROWS == 0 and rows >= NA_WIN_ROWS
    nrb = rows // NA_QROWS
    blk = NA_QROWS * GRID_W
    npair = NA_HEADS // 2

    def spec(col0, shift):
        return pl.BlockSpec((blk, LANES),
                            lambda hp, b, rb: (b * nrb + jnp.clip(rb + shift, 0, nrb - 1), col0 + hp))
    in_specs = [spec(0, 0), spec(npair, -1), spec(npair, 0), spec(npair, 1),
                spec(0, -1), spec(0, 0), spec(0, 1),
                pl.BlockSpec((1, NA_WIN_ROWS, 2 * GRID_W, NA_WIN_ROWS * GRID_W), lambda hp, b, rb: (hp, 0, 0, 0))]
    return pl.pallas_call(
        functools.partial(_na_body, rows=rows), grid=(npair, bsz, nrb), in_specs=in_specs,
        out_specs=spec(0, 0),
        out_shape=jax.ShapeDtypeStruct((bsz * seq_len, D_MODEL), BF16),
        scratch_shapes=[pltpu.VMEM((3 * blk, LANES), BF16), pltpu.VMEM((3 * blk, LANES), BF16)],
        compiler_params=_cparams(3), name="na_attention")(qk, qk, qk, qk, v, v, v, bt)


HG_CHUNK = 128
HG_BLOCK = 512
HG_PAIR = 4


def _bcast_row(x, group, r):
    C = x.shape[0]
    x3 = x.reshape(C // group, group, x.shape[1])
    return jnp.broadcast_to(x3[:, r:r + 1, :], x3.shape).reshape(x.shape)


def _hgrn_body(xq_ref, xi_ref, xf_ref, lb_ref, o_ref, st_ref, *, rev, nchunk):
    C = HG_CHUNK

    @pl.when(pl.program_id(2) == 0)
    def _():
        st_ref[...] = jnp.zeros_like(st_ref)

    row = lax.broadcasted_iota(I32, (C, C), 0)
    col = lax.broadcasted_iota(I32, (C, C), 1)
    tri = jnp.where((row <= col) if rev else (row >= col), 1.0, 0.0).astype(BF16)
    t_idx = lax.broadcasted_iota(I32, (C, HG_DIM), 0)
    sub = t_idx % SUBLANES

    def one_head(rws, hh):
        cols = slice(hh * HG_DIM, (hh + 1) * HG_DIM)
        lbv = lb_ref[:, cols]
        q = xq_ref[rws, cols] * (HG_DIM ** -0.5)
        xi = xi_ref[rws, cols]
        v = xi * _sigmoid(xi)
        f = lbv + (1.0 - lbv) * _sigmoid(xf_ref[rws, cols])
        k = 1.0 - f
        lf = jnp.log(f) * LOG2E
        lf_hi = lf.astype(BF16)
        lf_lo = (lf - lf_hi.astype(F32)).astype(BF16)
        b = (jnp.dot(tri, lf_hi, preferred_element_type=F32)
             + jnp.dot(tri, lf_lo, preferred_element_type=F32))
        vb = v.astype(BF16)
        state = st_ref[hh]
        o = jnp.dot((q * jnp.exp2(b)).astype(BF16), state.astype(BF16), preferred_element_type=F32)
        for s in range(SUBLANES):
            msk = (sub <= s) if rev else (sub >= s)
            e = jnp.exp2(jnp.where(msk, b - _bcast_row(b, SUBLANES, s), NEG))
            sc = jnp.sum(q * _bcast_row(k, SUBLANES, s) * e, axis=-1, keepdims=True)
            o = o + sc * _bcast_row(v, SUBLANES, s)
        scores = jnp.zeros((C, C), F32)
        m = SUBLANES
        while m < C:
            right = ((t_idx // m) % 2) == 1
            rr = _bcast_row(b, 2 * m, m if rev else m - 1)
            qside = jnp.logical_not(right) if rev else right
            kside = right if rev else jnp.logical_not(right)
            qe = jnp.where(qside, q * jnp.exp2(jnp.where(qside, b - rr, 0.0)), 0.0)
            ke = jnp.where(kside, k * jnp.exp2(jnp.where(kside, rr - b, 0.0)), 0.0)
            sl = lax.dot_general(qe.astype(BF16), ke.astype(BF16), NT, preferred_element_type=F32)
            scores = scores + jnp.where((row // (2 * m)) == (col // (2 * m)), sl, 0.0)
            m *= 2
        o = o + jnp.dot(scores.astype(BF16), vb, preferred_element_type=F32)
        o_ref[rws, cols] = o
        bl = b[0:1, :] if rev else b[C - 1:C, :]
        ke = (k * jnp.exp2(bl - b)).astype(BF16)
        upd = lax.dot_general(ke, vb, TN, preferred_element_type=F32)
        decay = jnp.transpose(jnp.broadcast_to(jnp.exp2(bl), (HG_DIM, HG_DIM)))
        st_ref[hh] = state * decay + upd

    def chunk(ci, carry):
        cc = (nchunk - 1 - ci) if rev else ci
        rws = pl.ds(pl.multiple_of(cc * C, C), C)
        for hh in range(HG_PAIR):
            one_head(rws, hh)
        return carry

    lax.fori_loop(0, nchunk, chunk, 0)


def _hgrn_direction(y, lb, bsz, seq_len, rev):
    lbk = min(HG_BLOCK, seq_len)
    assert seq_len % lbk == 0 and lbk % HG_CHUNK == 0
    nb = seq_len // lbk
    fpart = 3 if rev else 2
    npair = HG_HEADS // HG_PAIR
    width = HG_PAIR * HG_DIM

    def spec(part):
        return pl.BlockSpec((lbk, width),
                            lambda b, h, c: (b * nb + ((nb - 1 - c) if rev else c), part * npair + h))
    return pl.pallas_call(
        functools.partial(_hgrn_body, rev=rev, nchunk=lbk // HG_CHUNK),
        grid=(bsz, npair, nb),
        in_specs=[spec(0), spec(1), spec(fpart), pl.BlockSpec((1, width), lambda b, h, c: (0, h))],
        out_specs=spec(0),
        out_shape=jax.ShapeDtypeStruct((bsz * seq_len, D_MODEL), F32),
        scratch_shapes=[pltpu.VMEM((HG_PAIR, HG_DIM, HG_DIM), F32)],
        compiler_params=_cparams(3), name="hgrn_bwd" if rev else "hgrn_fwd")(y, y, y, lb.reshape(1, D_MODEL))


def _flash_body(q_ref, k_ref, v_ref, o_ref, m_sc, acc_sc):
    ki = pl.program_id(3)

    @pl.when(ki == 0)
    def _():
        m_sc[...] = jnp.full_like(m_sc, NEG)
        acc_sc[...] = jnp.zeros_like(acc_sc)

    lane = lax.broadcasted_iota(I32, (1, LANES), 1)
    lo = lane < 64
    zero = jnp.zeros((), BF16)
    k = k_ref[...]
    v1 = jnp.where(lo, v_ref[...], jnp.ones((), BF16))

    def scores(hd):
        qp = q_ref[:, (hd // 2) * LANES:(hd // 2 + 1) * LANES]
        qm = jnp.where(lo if hd % 2 == 0 else jnp.logical_not(lo), qp, zero)
        return lax.dot_general(qm, k, NT, preferred_element_type=F32)

    n_heads = GQA_HEADS // GQA_KV
    s_next = scores(0)
    for hd in range(n_heads):
        s = s_next
        if hd + 1 < n_heads:
            s_next = scores(hd + 1)
        m_prev = m_sc[hd]
        m_new = jnp.maximum(m_prev, jnp.max(s, axis=-1, keepdims=True))
        p = jnp.exp2(s - m_new)
        acc_sc[hd] = (jnp.exp2(m_prev - m_new) * acc_sc[hd]
                      + jnp.dot(p.astype(BF16), v1, preferred_element_type=F32))
        m_sc[hd] = m_new

    @pl.when(ki == pl.num_programs(3) - 1)
    def _():
        for pair in range(2):
            a0 = acc_sc[2 * pair]
            a1 = pltpu.roll(acc_sc[2 * pair + 1], 64, 1)
            o_ref[:, pair * LANES:(pair + 1) * LANES] = jnp.where(
                lo, a0 / a0[:, 64:65], a1 / a1[:, 0:1]).astype(o_ref.dtype)


def _flash_gqa(qk, v, bsz, seq_len):
    tq = min(256, seq_len)
    tk = min(8192, seq_len)
    nq, nk = seq_len // tq, seq_len // tk
    gw = (GQA_HEADS // GQA_KV) * GQA_DH
    return pl.pallas_call(
        _flash_body, grid=(bsz, GQA_KV, nq, nk),
        in_specs=[pl.BlockSpec((tq, gw), lambda b, g, i, j: (b * nq + i, g)),
                  pl.BlockSpec((tk, LANES), lambda b, g, i, j: (b * nk + j, D_MODEL // LANES + g)),
                  pl.BlockSpec((tk, LANES), lambda b, g, i, j: (b * nk + j, g))],
        out_specs=pl.BlockSpec((tq, gw), lambda b, g, i, j: (b * nq + i, g)),
        out_shape=jax.ShapeDtypeStruct((bsz * seq_len, D_MODEL), BF16),
        scratch_shapes=[pltpu.VMEM((4, tq, 1), F32), pltpu.VMEM((4, tq, LANES), F32)],
        compiler_params=_cparams(4), name="flash_gqa")(qk, qk, v)


RET_CHUNK = 128
RET_BLOCK = 512


def _ret_body(q_ref, k_ref, v_ref, dm_ref, qd_ref, kd_ref, cd_ref, o_ref, st_ref, *, rev, nchunk):
    C = RET_CHUNK

    @pl.when(pl.program_id(2) == 0)
    def _():
        st_ref[...] = jnp.zeros_like(st_ref)

    def chunk(ci, carry):
        cc = (nchunk - 1 - ci) if rev else ci
        rws = pl.ds(pl.multiple_of(cc * C, C), C)
        q = q_ref[rws, :]
        k = k_ref[rws, :]
        v = v_ref[rws, :]
        state = st_ref[...]
        s = lax.dot_general(q, k, NT, preferred_element_type=F32) * dm_ref[0]
        o = jnp.dot(s.astype(BF16), v, preferred_element_type=F32)
        qs = (q.astype(F32) * qd_ref[0]).astype(BF16)
        o = o + jnp.dot(qs, state.astype(BF16), preferred_element_type=F32)
        o_ref[rws, :] = o
        ks = (k.astype(F32) * kd_ref[0]).astype(BF16)
        st_ref[...] = state * cd_ref[0] + lax.dot_general(ks, v, TN, preferred_element_type=F32)
        return carry

    lax.fori_loop(0, nchunk, chunk, 0)


def _ret_tables(rev):
    C = RET_CHUNK
    log_gamma = jnp.log1p(-jnp.exp2(-5.0 - jnp.arange(RET_HEADS, dtype=F32)))[:, None, None]
    pos = jnp.arange(C, dtype=F32)
    rel = pos[:, None] - pos[None, :]
    if rev:
        rel = -rel
        qpow, kpow = C - pos, pos
    else:
        qpow, kpow = pos + 1.0, C - 1.0 - pos
    dm = jnp.where(rel[None] >= 0, jnp.exp(rel[None] * log_gamma), 0.0)
    qd = jnp.broadcast_to(jnp.exp(qpow[None, :, None] * log_gamma), (RET_HEADS, C, RET_DK))
    kd = jnp.broadcast_to(jnp.exp(kpow[None, :, None] * log_gamma), (RET_HEADS, C, RET_DK))
    cd = jnp.broadcast_to(jnp.exp(C * log_gamma), (RET_HEADS, 1, RET_DV))
    return dm.astype(F32), qd.astype(F32), kd.astype(F32), cd.astype(F32)


def _ret_direction(qk, v, bsz, seq_len, rev):
    lbk = min(RET_BLOCK, seq_len)
    nb = seq_len // lbk
    rowblk = lambda b, h, c: b * nb + ((nb - 1 - c) if rev else c)
    tab = lambda shape: pl.BlockSpec((1,) + shape, lambda b, h, c: (h, 0, 0))
    return pl.pallas_call(
        functools.partial(_ret_body, rev=rev, nchunk=lbk // RET_CHUNK),
        grid=(bsz, RET_HEADS, nb),
        in_specs=[pl.BlockSpec((lbk, RET_DK), lambda b, h, c: (rowblk(b, h, c), h)),
                  pl.BlockSpec((lbk, RET_DK), lambda b, h, c: (rowblk(b, h, c), RET_HEADS + h)),
                  pl.BlockSpec((lbk, RET_DV), lambda b, h, c: (rowblk(b, h, c), h)),
                  tab((RET_CHUNK, RET_CHUNK)), tab((RET_CHUNK, RET_DK)), tab((RET_CHUNK, RET_DK)),
                  tab((1, RET_DV))],
        out_specs=pl.BlockSpec((lbk, RET_DV), lambda b, h, c: (rowblk(b, h, c), h)),
        out_shape=jax.ShapeDtypeStruct((bsz * seq_len, RET_HEADS * RET_DV), F32),
        scratch_shapes=[pltpu.VMEM((RET_DK, RET_DV), F32)],
        compiler_params=_cparams(3), name="ret_bwd" if rev else "ret_fwd")(qk, qk, v, *_ret_tables(rev))


MOE_FF_CHUNK = 512


MOE_SLOTS = 3


def _moe_body(idx_ref, nx1_ref, nx2_ref, u_hbm, gate_ref, wg_ref, wu_ref, wd_ref, hi_ref, lo_ref, xbuf, sem, *, tc):
    nsteps = pl.num_programs(0) * pl.num_programs(1)
    step = pl.program_id(0) * pl.num_programs(1) + pl.program_id(1)
    slot = step % MOE_SLOTS

    def row_copy(token, r, s):
        src = u_hbm.at[pl.ds(pl.multiple_of(token * SUBLANES, SUBLANES), SUBLANES)]
        return pltpu.make_async_copy(src, xbuf.at[s, pl.ds(r * SUBLANES, SUBLANES)], sem.at[s])

    def wait_rows(s):
        for _ in range(tc):
            row_copy(0, 0, s).wait()

    @pl.when(step == 0)
    def _():
        def issue(r, carry):
            row_copy(idx_ref[0, 0, r], r, 0).start()
            row_copy(nx1_ref[0, 0, r], r, 1).start()
            return carry
        lax.fori_loop(0, tc, issue, 0)

    wait_rows(slot)
    x = jnp.concatenate([xbuf[slot, pl.ds(s, tc, stride=SUBLANES), :] for s in range(SUBLANES)],
                        axis=1).astype(BF16)
    ahead = (step + 2) % MOE_SLOTS
    for r in range(tc):
        row_copy(nx2_ref[0, 0, r], r, ahead).start()
    acc = jnp.zeros((tc, D_MODEL), F32)
    for f in range(EXPERT_FF // MOE_FF_CHUNK):
        cols = slice(f * MOE_FF_CHUNK, (f + 1) * MOE_FF_CHUNK)
        g = jnp.dot(x, wg_ref[0, 0, :, cols], preferred_element_type=F32)
        up = jnp.dot(x, wu_ref[0, 0, :, cols], preferred_element_type=F32)
        hid = ((g * _sigmoid(g)) * up).astype(BF16)
        acc = acc + jnp.dot(hid, wd_ref[0, 0, cols, :], preferred_element_type=F32)
    gate = gate_ref[0]
    for c in range(D_MODEL // LANES):
        cols = slice(c * LANES, (c + 1) * LANES)
        ye = acc[:, cols] * gate
        hi = ye.astype(BF16)
        hi_ref[0, :, cols] = hi
        lo_ref[0, :, cols] = (ye - hi.astype(F32)).astype(BF16)

    @pl.when(step == nsteps - 1)
    def _():
        wait_rows((step + 1) % MOE_SLOTS)
        wait_rows(ahead)


def _moe_experts(u, idx_sorted, gate_sorted, wg, wu, wd, layer):
    cap = idx_sorted.shape[1]
    tc = min(512, cap)
    nj = cap // tc
    nsteps = N_EXPERTS * nj
    assert nsteps >= MOE_SLOTS
    idx3 = idx_sorted.reshape(nsteps, 1, tc)
    gate_b = jnp.broadcast_to(gate_sorted[:, :, None], (N_EXPERTS, cap, LANES))
    ispec = lambda d: pl.BlockSpec((1, 1, tc), lambda e, j: (jnp.minimum(e * nj + j + d, nsteps - 1), 0, 0),
                                   memory_space=pltpu.SMEM)
    wspec = lambda shape: pl.BlockSpec((1, 1) + shape, lambda e, j: (layer, e, 0, 0))
    ospec = pl.BlockSpec((1, tc, D_MODEL), lambda e, j: (e, j, 0))
    oshape = jax.ShapeDtypeStruct((N_EXPERTS, cap, D_MODEL), BF16)
    return pl.pallas_call(
        functools.partial(_moe_body, tc=tc), grid=(N_EXPERTS, nj),
        in_specs=[ispec(0), ispec(1), ispec(2),
                  pl.BlockSpec(memory_space=pl.ANY),
                  pl.BlockSpec((1, tc, LANES), lambda e, j: (e, j, 0)),
                  wspec((D_MODEL, EXPERT_FF)), wspec((D_MODEL, EXPERT_FF)), wspec((EXPERT_FF, D_MODEL))],
        out_specs=[ospec, ospec], out_shape=[oshape, oshape],
        scratch_shapes=[pltpu.VMEM((MOE_SLOTS, tc * SUBLANES, LANES), F32),
                        pltpu.SemaphoreType.DMA((MOE_SLOTS,))],
        compiler_params=_cparams(2), name="moe_experts")(idx3, idx3, idx3, u, gate_b, wg, wu, wd)


CMB_TOKENS = 256
CMB_WINDOW = LANES // 2
CMB_ALIGN = 2 * SUBLANES


def _combine_body(base_ref, nr_ref, h_ref, pos_ref, hi_hbm, lo_hbm, o_ref, buf, xbuf, sem, xsem, *, cap, ntiles):
    i = pl.program_id(0)
    slot = i % 2
    Wn = CMB_WINDOW
    lane = lax.broadcasted_iota(I32, (1, LANES), 1)
    lo_half = lane < Wn

    def window(tile, e, k):
        start = (base_ref[e * ntiles + tile] // CMB_ALIGN) * CMB_ALIGN + k * Wn
        w0 = pl.multiple_of(jnp.minimum(start, cap - Wn), CMB_ALIGN)
        return start, w0

    def copies(tile, k, dst, dsem):
        out = []
        for e in range(N_EXPERTS):
            w0 = window(tile, e, k)[1]
            out.append(pltpu.make_async_copy(hi_hbm.at[e, pl.ds(w0, Wn)], dst.at[0, e], dsem))
            out.append(pltpu.make_async_copy(lo_hbm.at[e, pl.ds(w0, Wn)], dst.at[1, e], dsem))
        return out

    def one_hot(k, check_start):
        parts = []
        for e in range(0, N_EXPERTS, 2):
            s0, w0 = window(i, e, k)
            s1, w1 = window(i, e + 1, k)
            p0 = pos_ref[:, e:e + 1]
            p1 = pos_ref[:, e + 1:e + 2]
            hit = jnp.where(lo_half, p0 - w0, p1 - w1 + Wn) == lane
            if check_start:
                hit = jnp.logical_and(hit, jnp.where(lo_half, p0 - s0, p1 - s1) >= 0)
            parts.append(jnp.where(hit, 1.0, 0.0).astype(BF16))
        return jnp.concatenate(parts, axis=1)

    def placed(smat, src):
        return (jnp.dot(smat, src[0].reshape(N_EXPERTS * Wn, D_MODEL), preferred_element_type=F32)
                + jnp.dot(smat, src[1].reshape(N_EXPERTS * Wn, D_MODEL), preferred_element_type=F32))

    @pl.when(i == 0)
    def _():
        for c in copies(0, 0, buf.at[0], sem.at[0]):
            c.start()

    @pl.when(i + 1 < ntiles)
    def _():
        for c in copies(i + 1, 0, buf.at[1 - slot], sem.at[1 - slot]):
            c.start()

    smat = one_hot(0, False)
    for c in copies(i, 0, buf.at[slot], sem.at[slot]):
        c.wait()
    acc = h_ref[...] + placed(smat, buf[slot])

    def extra_round(k, acc):
        for c in copies(i, k, xbuf, xsem):
            c.start()
        smat = one_hot(k, True)
        for c in copies(i, k, xbuf, xsem):
            c.wait()
        return acc + placed(smat, xbuf[...])

    o_ref[...] = lax.fori_loop(1, nr_ref[i], extra_round, acc)


def _moe_combine(h, ye_hi, ye_lo, pos_t, base, nrounds):
    T = h.shape[0]
    cap = ye_hi.shape[1]
    ntiles = T // CMB_TOKENS
    assert cap >= CMB_WINDOW and cap % CMB_ALIGN == 0
    wshape = (2, N_EXPERTS, CMB_WINDOW, D_MODEL)
    grid_spec = pltpu.PrefetchScalarGridSpec(
        num_scalar_prefetch=2, grid=(ntiles,),
        in_specs=[pl.BlockSpec((CMB_TOKENS, D_MODEL), lambda i, b, n: (i, 0)),
                  pl.BlockSpec((CMB_TOKENS, N_EXPERTS), lambda i, b, n: (i, 0)),
                  pl.BlockSpec(memory_space=pl.ANY), pl.BlockSpec(memory_space=pl.ANY)],
        out_specs=pl.BlockSpec((CMB_TOKENS, D_MODEL), lambda i, b, n: (i, 0)),
        scratch_shapes=[pltpu.VMEM((2,) + wshape, BF16), pltpu.VMEM(wshape, BF16),
                        pltpu.SemaphoreType.DMA((2,)), pltpu.SemaphoreType.DMA(())])
    return pl.pallas_call(
        functools.partial(_combine_body, cap=cap, ntiles=ntiles), grid_spec=grid_spec,
        out_shape=jax.ShapeDtypeStruct((T, D_MODEL), F32),
        compiler_params=_cparams(1), name="moe_combine")(base.reshape(-1), nrounds, h, pos_t, ye_hi, ye_lo)


SEL_ROWS = 4


def _flag(cond):
    return jnp.where(cond, 1.0, 0.0)


def _select_body(aff_ref, pos_ref, idx_ref, gate_ref, thr_sc, *, cap):
    n_exp, n_chunk, _ = aff_ref.shape

    def bit_step(i, prefix):
        cand = prefix | jnp.left_shift(jnp.int32(1), 30 - i)
        hit = _flag(pltpu.bitcast(aff_ref[...], I32) >= cand)
        cnt = jnp.sum(jnp.sum(hit, axis=1, keepdims=True), axis=2, keepdims=True)
        return jnp.where(cnt >= cap, cand, prefix)

    thr = lax.fori_loop(0, 31, bit_step, jnp.zeros((n_exp, 1, 1), I32))
    thr_sc[...] = jnp.broadcast_to(thr, thr_sc.shape)

    li = lax.broadcasted_iota(I32, (LANES, LANES), 0)
    lj = lax.broadcasted_iota(I32, (LANES, LANES), 1)
    upper = _flag(li <= lj).astype(BF16)
    ci = lax.broadcasted_iota(I32, (n_chunk, n_chunk), 0)
    cj = lax.broadcasted_iota(I32, (n_chunk, n_chunk), 1)
    before = _flag(cj < ci).astype(BF16)
    chunk_id = lax.broadcasted_iota(I32, (n_chunk, LANES), 0).astype(F32)
    lane_id = lax.broadcasted_iota(I32, (LANES, LANES), 0).astype(F32)
    slot_lane = lax.broadcasted_iota(I32, (1, LANES), 1)

    def counts(flags):
        local = jnp.dot(flags.astype(BF16), upper, preferred_element_type=F32)
        total = jnp.broadcast_to(local[:, LANES - 1:LANES], local.shape)
        return local, total, jnp.dot(before, total.astype(BF16), preferred_element_type=F32)

    def per_expert(e, carry):
        aff = aff_ref[e]
        key = pltpu.bitcast(aff, I32)
        t = thr_sc[e]
        gt = _flag(key > t)
        eq = _flag(key == t)
        need = cap - jnp.sum(jnp.sum(gt, axis=1, keepdims=True), axis=0, keepdims=True)
        eq_local, _, eq_off = counts(eq)
        sel = gt + eq * _flag(eq_local + eq_off - eq < need)
        local, total, off = counts(sel)
        pos_ref[e] = jnp.where(sel > 0.0, local + off - 1.0, -1.0).astype(I32)
        reached = off + total
        local_t = jnp.transpose(local).astype(BF16)
        aff_tr = jnp.transpose(aff)

        def one_row(r):
            slot = (r * LANES + slot_lane).astype(F32)
            chunk = jnp.sum(_flag(reached <= slot), axis=0, keepdims=True)
            skipped = jnp.sum(jnp.where(chunk_id < chunk, total, 0.0), axis=0, keepdims=True)
            pick = _flag(chunk_id == chunk)
            run = jnp.dot(local_t, pick.astype(BF16), preferred_element_type=F32)
            lane = jnp.sum(_flag(run <= slot - skipped), axis=0, keepdims=True)
            idx_ref[e, pl.ds(r, 1), :] = (chunk * LANES + lane).astype(I32)
            vals = jnp.dot(aff_tr, pick, precision=lax.Precision.HIGHEST, preferred_element_type=F32)
            gate_ref[e, pl.ds(r, 1), :] = jnp.sum(jnp.where(lane_id == lane, vals, 0.0), axis=0, keepdims=True)

        def row_group(g, c2):
            for i in range(SEL_ROWS):
                one_row(g * SEL_ROWS + i)
            return c2

        lax.fori_loop(0, cap // (LANES * SEL_ROWS), row_group, 0)
        return carry

    lax.fori_loop(0, n_exp, per_expert, 0)


def _select(aff_t, cap):
    E, T = aff_t.shape
    assert T % LANES == 0 and cap % (LANES * SEL_ROWS) == 0
    full = lambda shape: pl.BlockSpec(shape, lambda i: (0, 0, 0))
    shapes = [(E, T // LANES, LANES), (E, cap // LANES, LANES), (E, cap // LANES, LANES)]
    pos, idx, gate = pl.pallas_call(
        functools.partial(_select_body, cap=cap), grid=(1,),
        in_specs=[full(shapes[0])], out_specs=[full(s) for s in shapes],
        out_shape=[jax.ShapeDtypeStruct(shapes[0], I32), jax.ShapeDtypeStruct(shapes[1], I32),
                   jax.ShapeDtypeStruct(shapes[2], F32)],
        scratch_shapes=[pltpu.VMEM((E, 1, LANES), I32)],
        compiler_params=_cparams(1), name="moe_select")(aff_t.reshape(shapes[0]))
    return pos.reshape(E, T), idx.reshape(E, cap), gate.reshape(E, cap)


def _route(aff_t, cap):
    E, T = aff_t.shape
    ntiles = T // CMB_TOKENS
    pos, idx_s, gate_s = _select(aff_t, cap)
    cnt = jnp.sum((pos >= 0).reshape(E, ntiles, CMB_TOKENS), axis=-1, dtype=I32)
    base = jnp.cumsum(cnt, axis=1, dtype=I32) - cnt
    need = jnp.where(cnt > 0, (base % CMB_ALIGN + cnt + CMB_WINDOW - 1) // CMB_WINDOW, 0)
    return idx_s, gate_s, pos.T, base, jnp.max(need, axis=0).astype(I32)


def _moe_layer(h, u, aff_t, wg, wu, wd, layer):
    T = h.shape[0]
    cap = CAPACITY_FACTOR * T // N_EXPERTS
    idx_s, gate_s, pos_t, base, nrounds = _route(aff_t, cap)
    ye_hi, ye_lo = _moe_experts(u, idx_s, gate_s, wg, wu, wd, layer)
    return _moe_combine(h, ye_hi, ye_lo, pos_t, base, nrounds)


def _rope_tables(seq_len, n_freq, reps):
    t = jnp.arange(seq_len)
    inv_freq = ROPE_THETA ** (-jnp.arange(n_freq, dtype=F32) / n_freq)
    out = []
    for pos in ((t // GRID_W).astype(F32), (t % GRID_W).astype(F32)):
        ang = pos[:, None] * inv_freq[None, :]
        c = jnp.concatenate([jnp.cos(ang), jnp.cos(ang)], axis=1)
        s = jnp.concatenate([-jnp.sin(ang), jnp.sin(ang)], axis=1)
        out.append((c, s))
    cos = jnp.concatenate([out[0][0], out[1][0]], axis=1)
    sin = jnp.concatenate([out[0][1], out[1][1]], axis=1)
    return jnp.tile(cos, (1, reps)), jnp.tile(sin, (1, reps))


def _dup_heads(w, n_heads, dh):
    k = w.shape[0]
    return jnp.repeat(w.reshape(k, n_heads, 1, dh), 2, axis=2).reshape(k, n_heads * 2 * dh)


def _trunk(x, p):
    bsz, seq_len, _ = x.shape
    T = bsz * seq_len
    h = x.reshape(T, D_MODEL)
    for layer in range(4):
        nm = p["norm_mix"][layer]
        if layer == 0:
            qk = _proj(h, nm, p["na_wqk"], seq_len, BF16, headnorm=True, colvec=p["na_qk_gain"], name="na_proj_qk")
            v = _proj(h, nm, p["na_wv"], seq_len, BF16, name="na_proj_v")
            mix_args = (_na_attention(qk, v, p["na_bt"], bsz, seq_len),)
            kind, w_out = "plain", p["na_wo"]
        elif layer == 1:
            y = _proj(h, nm, p["hg_win"], seq_len, F32, name="hg_proj")
            o_f = _hgrn_direction(y, p["hg_lb"], bsz, seq_len, False)
            o_b = _hgrn_direction(y, p["hg_lb"], bsz, seq_len, True)
            mix_args = (o_f, o_b, y, 4, p["hg_o_gain"])
            kind, w_out = "hg", p["hg_wo"]
        elif layer == 2:
            cos, sin = _rope_tables(seq_len, 16, 2)
            qk = _proj(h, nm, p["gq_wqk"], seq_len, BF16, headnorm=True, colvec=p["gq_qk_gain"],
                       rope=(16, cos, sin), name="gqa_proj_qk")
            v = _proj(h, nm, p["gq_wv"], seq_len, BF16, name="gqa_proj_v")
            mix_args = (_flash_gqa(qk, v, bsz, seq_len),)
            kind, w_out = "plain", p["gq_wo"]
        else:
            cos, sin = _rope_tables(seq_len, 64, 1)
            qk = _proj(h, nm, p["rt_wqk"], seq_len, BF16, colvec=p["rt_qk_scale"],
                       rope=(64, cos, sin), name="ret_proj_qk")
            v = _proj(h, nm, p["rt_wv"], seq_len, BF16, name="ret_proj_v")
            g = _proj(h, nm, p["rt_wg"], seq_len, F32, name="ret_proj_g")
            o_f = _ret_direction(qk, v, bsz, seq_len, False)
            o_b = _ret_direction(qk, v, bsz, seq_len, True)
            mix_args = (o_f, o_b, g, 0, p["rt_o_gain"])
            kind, w_out = "rt", p["rt_wo"]
        h, u, aff_t = _outproj(h, mix_args, w_out, p["norm_ffn"][layer], p["router_t"][layer], kind,
                               name=f"outproj_{kind}")
        h = _moe_layer(h, u, aff_t, p["moe_wg"], p["moe_wu"], p["moe_wd"], layer)
    return h.reshape(bsz, seq_len, D_MODEL)


def kernel(x_prompt, x_sample, norm_mix, norm_ffn, na_w_in, na_q_gain, na_k_gain, na_rel_bias, na_w_out, hg_w_in, hg_lb, hg_o_gain, hg_w_out, gq_w_in, gq_q_gain, gq_k_gain, gq_w_out, rt_w_in, rt_o_gain, rt_w_out, moe_router, moe_w_gate, moe_w_up, moe_w_down):
    bf = lambda a: a.astype(BF16)
    na_w, gq_w, rt_w = na_w_in[0], gq_w_in[0], rt_w_in[0]
    lb_cum = jnp.cumsum(jax.nn.softmax(hg_lb.astype(F32), axis=0), axis=0)
    qd = GQA_HEADS * GQA_DH
    kd = GQA_KV * GQA_DH
    rq = RET_HEADS * RET_DK
    rv = RET_HEADS * RET_DV
    p = {
        "norm_mix": norm_mix, "norm_ffn": norm_ffn,
        "na_wqk": bf(na_w[:, :2 * D_MODEL]), "na_wv": bf(na_w[:, 2 * D_MODEL:]),
        "na_qk_gain": jnp.concatenate([jnp.tile(na_q_gain[0], NA_HEADS) * (64 ** -0.5 * LOG2E),
                                       jnp.tile(na_k_gain[0], NA_HEADS)]),
        "na_bt": _na_bias_table(na_rel_bias[0]), "na_wo": bf(na_w_out[0]),
        "hg_win": bf(hg_w_in[0]), "hg_lb": lb_cum[1] - lb_cum[0], "hg_o_gain": hg_o_gain[0],
        "hg_wo": bf(hg_w_out[0]),
        "gq_wqk": bf(jnp.concatenate([gq_w[:, :qd], _dup_heads(gq_w[:, qd:qd + kd], GQA_KV, GQA_DH)], axis=1)),
        "gq_wv": bf(_dup_heads(gq_w[:, qd + kd:], GQA_KV, GQA_DH)),
        "gq_qk_gain": jnp.concatenate([jnp.tile(gq_q_gain[0], GQA_HEADS) * (GQA_DH ** -0.5 * LOG2E),
                                       jnp.tile(gq_k_gain[0], 2 * GQA_KV)]),
        "gq_wo": bf(gq_w_out[0]),
        "rt_wqk": bf(rt_w[:, :2 * rq]), "rt_wv": bf(rt_w[:, 2 * rq:2 * rq + rv]), "rt_wg": bf(rt_w[:, 2 * rq + rv:]),
        "rt_qk_scale": jnp.concatenate([jnp.full((rq,), RET_DK ** -0.5, F32), jnp.ones((rq,), F32)]),
        "rt_o_gain": rt_o_gain[0], "rt_wo": bf(rt_w_out[0]),
        "router_t": jnp.swapaxes(moe_router, 1, 2).astype(F32),
        "moe_wg": bf(moe_w_gate), "moe_wu": bf(moe_w_up), "moe_wd": bf(moe_w_down),
    }
    return (_trunk(x_prompt, p), _trunk(x_sample, p))
```

```python
import functools
import math

import jax
import jax.numpy as jnp
from jax import lax
from jax.experimental import pallas as pl
from jax.experimental.pallas import tpu as pltpu

F32 = jnp.float32
BF16 = jnp.bfloat16
I32 = jnp.int32

D_MODEL = 1024
GRID_W = 64
EPS = 1e-6
ROPE_THETA = 10000.0
NEG = -1e30
LOG2E = math.log2(math.e)

NA_HEADS = 16
NA_WIN_ROWS = 8
NA_WIN_COLS = 16
HG_HEADS = 8
HG_DIM = 128
GQA_HEADS = 16
GQA_KV = 4
GQA_DH = 64
RET_HEADS = 4
RET_DK = 256
RET_DV = 512
N_EXPERTS = 16
EXPERT_FF = 2048
CAPACITY_FACTOR = 2

LANES = 128
SUBLANES = 8
VMEM_LIMIT = 52 * 1024 * 1024

NT = (((1,), (1,)), ((), ()))
TN = (((0,), (0,)), ((), ()))


def _cparams(n_axes):
    return pltpu.CompilerParams(dimension_semantics=("arbitrary",) * n_axes,
                                vmem_limit_bytes=VMEM_LIMIT)


def _sigmoid(x):
    return 1.0 / (1.0 + jnp.exp(-x))


def _proj_body(*refs, headnorm, has_cv, rope_k, tw):
    it = iter(refs)
    x_ref, g_ref, w_ref = next(it), next(it), next(it)
    cv_ref = next(it) if has_cv else None
    cos_ref = next(it) if rope_k else None
    sin_ref = next(it) if rope_k else None
    o_ref, xn_ref = next(it), next(it)

    @pl.when(pl.program_id(1) == 0)
    def _():
        x = x_ref[...]
        ms = jnp.mean(x * x, axis=-1, keepdims=True)
        xn_ref[...] = (x * lax.rsqrt(ms + EPS) * g_ref[...]).astype(BF16)

    acc = jnp.dot(xn_ref[...], w_ref[pl.program_id(1)], preferred_element_type=F32)
    if not (headnorm or has_cv or rope_k):
        o_ref[...] = acc.astype(o_ref.dtype)
        return
    lane = lax.broadcasted_iota(I32, (1, LANES), 1)
    if headnorm:
        same_head = _flag((lax.broadcasted_iota(I32, (LANES, LANES), 0) < 64)
                          == (lax.broadcasted_iota(I32, (LANES, LANES), 1) < 64)).astype(BF16)
    for c in range(acc.shape[1] // LANES):
        cols = slice(c * LANES, (c + 1) * LANES)
        a = acc[:, cols]
        if headnorm:
            sq = a * a
            sq_hi = sq.astype(BF16)
            sq_lo = (sq - sq_hi.astype(F32)).astype(BF16)
            ssq = (jnp.dot(sq_hi, same_head, preferred_element_type=F32)
                   + jnp.dot(sq_lo, same_head, preferred_element_type=F32))
            a = a * lax.rsqrt(ssq * (1.0 / 64.0) + EPS)
        if has_cv:
            a = a * cv_ref[:, cols]
        if rope_k:
            tcol = (c * LANES) % tw
            if rope_k == 64:
                partner = pltpu.roll(a, 64, 1)
            else:
                partner = jnp.where((lane & rope_k) != 0, pltpu.roll(a, rope_k, 1),
                                    pltpu.roll(a, LANES - rope_k, 1))
            a = a * cos_ref[:, tcol:tcol + LANES] + partner * sin_ref[:, tcol:tcol + LANES]
        o_ref[:, cols] = a.astype(o_ref.dtype)


def _proj(x, g, w, seq_len, out_dtype, *, headnorm=False, colvec=None, rope=None, name="proj"):
    T = x.shape[0]
    N = w.shape[1]
    tm = min(1024, seq_len)
    tn = 512 if N % 512 == 0 else 256
    assert T % tm == 0 and seq_len % tm == 0 and N % tn == 0
    nsb = seq_len // tm
    in_specs = [pl.BlockSpec((tm, D_MODEL), lambda i, j: (i, 0)),
                pl.BlockSpec((1, D_MODEL), lambda i, j: (0, 0)),
                pl.BlockSpec((N // tn, D_MODEL, tn), lambda i, j: (0, 0, 0))]
    w_tiles = jnp.swapaxes(w.reshape(D_MODEL, N // tn, tn), 0, 1)
    args = [x, g.reshape(1, D_MODEL).astype(F32), w_tiles]
    if colvec is not None:
        in_specs.append(pl.BlockSpec((1, tn), lambda i, j: (0, j)))
        args.append(colvec.reshape(1, N).astype(F32))
    rope_k, tw = 0, LANES
    if rope is not None:
        rope_k, cos, sin = rope
        tw = cos.shape[1]
        assert tn % tw == 0
        in_specs += [pl.BlockSpec((tm, tw), lambda i, j: (i % nsb, 0))] * 2
        args += [cos, sin]
    body = functools.partial(_proj_body, headnorm=headnorm, has_cv=colvec is not None,
                             rope_k=rope_k, tw=tw)
    return pl.pallas_call(
        body, grid=(T // tm, N // tn), in_specs=in_specs,
        out_specs=pl.BlockSpec((tm, tn), lambda i, j: (i, j)),
        out_shape=jax.ShapeDtypeStruct((T, N), out_dtype),
        scratch_shapes=[pltpu.VMEM((tm, D_MODEL), BF16)],
        compiler_params=_cparams(2), name=name)(*args)


def _group_rms(o, gain_ref, width):
    parts = []
    for c in range(o.shape[1] // width):
        a = o[:, c * width:(c + 1) * width]
        ms = jnp.mean(a * a, axis=-1, keepdims=True)
        parts.append(a * lax.rsqrt(ms + EPS) * gain_ref[...])
    return jnp.concatenate(parts, axis=1)


OUTPROJ_SPLIT = 2


def _outproj_body(*refs, kind):
    it = iter(refs)
    h_ref = next(it)
    if kind == "plain":
        a_ref = next(it)
    else:
        of_ref, ob_ref, g_ref, gain_ref = next(it), next(it), next(it), next(it)
    w_ref, nf_ref, wr_ref = next(it), next(it), next(it)
    hn_ref, u_ref, aff_ref = next(it), next(it), next(it)
    hm = h_ref.shape[0] // OUTPROJ_SPLIT
    for part in range(OUTPROJ_SPLIT):
        rows = slice(part * hm, (part + 1) * hm)
        if kind == "plain":
            a = a_ref[rows, :]
        else:
            o = of_ref[rows, :] + ob_ref[rows, :]
            g = g_ref[rows, :]
            if kind == "hg":
                a = (_group_rms(o, gain_ref, HG_DIM) * _sigmoid(g)).astype(BF16)
            else:
                a = ((g * _sigmoid(g)) * _group_rms(o, gain_ref, RET_DV)).astype(BF16)
        hn = h_ref[rows, :] + jnp.dot(a, w_ref[...], preferred_element_type=F32)
        hn_ref[rows, :] = hn
        ms = jnp.mean(hn * hn, axis=-1, keepdims=True)
        u = hn * lax.rsqrt(ms + EPS) * nf_ref[...]
        for s in range(SUBLANES):
            u_ref[pl.ds(part * hm * SUBLANES + s, hm, stride=SUBLANES), :] = u[:, s * LANES:(s + 1) * LANES]
        logits = lax.dot_general(wr_ref[...], u, NT, precision=lax.Precision.HIGHEST,
                                 preferred_element_type=F32)
        m = jnp.max(logits, axis=0, keepdims=True)
        e = jnp.exp(logits - m)
        aff_ref[:, rows] = e / jnp.sum(e, axis=0, keepdims=True)


def _outproj(h, mix_args, w_out, norm_g, w_router_t, kind, name):
    T = h.shape[0]
    tm = 256 if kind == "rt" else 512
    K = w_out.shape[0]
    row = lambda i: (i, 0)
    in_specs = [pl.BlockSpec((tm, D_MODEL), row)]
    args = [h]
    if kind == "plain":
        (a,) = mix_args
        in_specs.append(pl.BlockSpec((tm, K), row))
        args.append(a)
    else:
        o_f, o_b, g_arr, g_col, gain = mix_args
        gw = gain.shape[0]
        in_specs += [pl.BlockSpec((tm, K), row), pl.BlockSpec((tm, K), row),
                     pl.BlockSpec((tm, K), lambda i: (i, g_col)),
                     pl.BlockSpec((1, gw), lambda i: (0, 0))]
        args += [o_f, o_b, g_arr, gain.reshape(1, gw).astype(F32)]
    in_specs += [pl.BlockSpec((K, D_MODEL), lambda i: (0, 0)),
                 pl.BlockSpec((1, D_MODEL), lambda i: (0, 0)),
                 pl.BlockSpec((N_EXPERTS, D_MODEL), lambda i: (0, 0))]
    args += [w_out, norm_g.reshape(1, D_MODEL).astype(F32), w_router_t]
    return pl.pallas_call(
        functools.partial(_outproj_body, kind=kind), grid=(T // tm,), in_specs=in_specs,
        out_specs=[pl.BlockSpec((tm, D_MODEL), row), pl.BlockSpec((tm * SUBLANES, LANES), row),
                   pl.BlockSpec((N_EXPERTS, tm), lambda i: (0, i))],
        out_shape=[jax.ShapeDtypeStruct((T, D_MODEL), F32), jax.ShapeDtypeStruct((T * SUBLANES, LANES), F32),
                   jax.ShapeDtypeStruct((N_EXPERTS, T), F32)],
        compiler_params=_cparams(1), name=name)(*args)


NA_QROWS = 8
NA_GROUP = 2


def _na_body(q_ref, kp_ref, kc_ref, kn_ref, vp_ref, vc_ref, vn_ref, bt_ref, o_ref, kbuf, vbuf, *, rows):
    rb = pl.program_id(2)
    blk = NA_QROWS * GRID_W
    for n, (kr, vr) in enumerate(((kp_ref, vp_ref), (kc_ref, vc_ref), (kn_ref, vn_ref))):
        kbuf[n * blk:(n + 1) * blk, :] = kr[...]
        vbuf[n * blk:(n + 1) * blk, :] = vr[...]
    lane = lax.broadcasted_iota(I32, (1, LANES), 1)
    lo = lane < 64
    zero = jnp.zeros((), BF16)

    def scores(unit):
        qi, gi = unit
        cols = slice(gi * LANES, (gi + 1) * LANES)
        r = rb * NA_QROWS + qi
        r_start = jnp.clip(r - NA_WIN_ROWS // 2, 0, rows - NA_WIN_ROWS)
        roff0 = r_start - r + (NA_WIN_ROWS - 1)
        off = pl.multiple_of((r_start - rb * NA_QROWS + NA_QROWS) * GRID_W, GRID_W)
        kw = kbuf[pl.ds(off, NA_WIN_ROWS * GRID_W), cols]
        qp = q_ref[qi * GRID_W:(qi + 1) * GRID_W, cols]
        q2 = jnp.concatenate([jnp.where(lo, qp, zero), jnp.where(lo, zero, qp)], axis=0)
        return lax.dot_general(q2, kw, NT, preferred_element_type=F32) + bt_ref[gi, roff0], off

    units = [(qi, gi) for qi in range(NA_QROWS) for gi in range(NA_GROUP)]
    nxt = scores(units[0])
    for n, (qi, gi) in enumerate(units):
        s, off = nxt
        if n + 1 < len(units):
            nxt = scores(units[n + 1])
        cols = slice(gi * LANES, (gi + 1) * LANES)
        p = jnp.exp2(s - jnp.max(s, axis=-1, keepdims=True))
        l = jnp.sum(p, axis=-1, keepdims=True)
        o = jnp.dot(p.astype(BF16), vbuf[pl.ds(off, NA_WIN_ROWS * GRID_W), cols], preferred_element_type=F32) / l
        o_ref[qi * GRID_W:(qi + 1) * GRID_W, cols] = jnp.where(lo, o[:GRID_W], o[GRID_W:]).astype(o_ref.dtype)


def _na_bias_table(rel_bias):
    c = jnp.arange(GRID_W)
    c_start = jnp.clip(c - NA_WIN_COLS // 2, 0, GRID_W - NA_WIN_COLS)
    kc = jnp.arange(GRID_W)
    valid = (kc[None, :] >= c_start[:, None]) & (kc[None, :] < c_start[:, None] + NA_WIN_COLS)
    coff = jnp.clip(kc[None, :] - c[:, None] + (NA_WIN_COLS - 1), 0, 2 * NA_WIN_COLS - 2)
    onehot = jnp.logical_and(coff[None] == jnp.arange(2 * NA_WIN_COLS - 1)[:, None, None], valid[None])
    bm = jnp.einsum("hrj,jck->hrck", rel_bias * LOG2E, onehot.astype(F32), precision=lax.Precision.HIGHEST)
    bm = bm + jnp.where(valid, 0.0, NEG)[None, None]
    bt = jnp.stack([jnp.concatenate([bm[:, r0 + i] for i in range(NA_WIN_ROWS)], axis=-1)
                    for r0 in range(NA_WIN_ROWS)], axis=1).astype(F32)
    bt = bt.reshape(NA_HEADS // 2, 2, NA_WIN_ROWS, GRID_W, NA_WIN_ROWS * GRID_W)
    return jnp.swapaxes(bt, 1, 2).reshape(NA_HEADS // 2, NA_WIN_ROWS, 2 * GRID_W, NA_WIN_ROWS * GRID_W)


def _na_attention(qk, v, bt, bsz, seq_len):
    rows = seq_len // GRID_W
    assert rows % NA_QROWS == 0 and rows >= NA_WIN_ROWS
    nrb = rows // NA_QROWS
    blk = NA_QROWS * GRID_W
    ngrp = NA_HEADS // (2 * NA_GROUP)
    width = NA_GROUP * LANES

    def spec(col0, shift):
        return pl.BlockSpec((blk, width),
                            lambda hp, b, rb: (b * nrb + jnp.clip(rb + shift, 0, nrb - 1), col0 + hp))
    in_specs = [spec(0, 0), spec(ngrp, -1), spec(ngrp, 0), spec(ngrp, 1),
                spec(0, -1), spec(0, 0), spec(0, 1),
                pl.BlockSpec((NA_GROUP, NA_WIN_ROWS, 2 * GRID_W, NA_WIN_ROWS * GRID_W),
                             lambda hp, b, rb: (hp, 0, 0, 0))]
    return pl.pallas_call(
        functools.partial(_na_body, rows=rows), grid=(ngrp, bsz, nrb), in_specs=in_specs,
        out_specs=spec(0, 0),
        out_shape=jax.ShapeDtypeStruct((bsz * seq_len, D_MODEL), BF16),
        scratch_shapes=[pltpu.VMEM((3 * blk, width), BF16), pltpu.VMEM((3 * blk, width), BF16)],
        compiler_params=_cparams(3), name="na_attention")(qk, qk, qk, qk, v, v, v, bt)


HG_CHUNK = 128
HG_BLOCK = 512
HG_PAIR = 4


def _bcast_row(x, group, r):
    C = x.shape[0]
    x3 = x.reshape(C // group, group, x.shape[1])
    return jnp.broadcast_to(x3[:, r:r + 1, :], x3.shape).reshape(x.shape)


def _hgrn_body(xq_ref, xi_ref, xf_ref, lb_ref, o_ref, st_ref, *, rev, nchunk):
    C = HG_CHUNK

    @pl.when(pl.program_id(2) == 0)
    def _():
        st_ref[...] = jnp.zeros_like(st_ref)

    row = lax.broadcasted_iota(I32, (C, C), 0)
    col = lax.broadcasted_iota(I32, (C, C), 1)
    tri = jnp.where((row <= col) if rev else (row >= col), 1.0, 0.0).astype(BF16)
    t_idx = lax.broadcasted_iota(I32, (C, HG_DIM), 0)
    sub = t_idx % SUBLANES

    def one_head(rws, hh):
        cols = slice(hh * HG_DIM, (hh + 1) * HG_DIM)
        lbv = lb_ref[:, cols]
        q = xq_ref[rws, cols] * (HG_DIM ** -0.5)
        xi = xi_ref[rws, cols]
        v = xi * _sigmoid(xi)
        f = lbv + (1.0 - lbv) * _sigmoid(xf_ref[rws, cols])
        k = 1.0 - f
        lf = jnp.log(f) * LOG2E
        lf_hi = lf.astype(BF16)
        lf_lo = (lf - lf_hi.astype(F32)).astype(BF16)
        b = (jnp.dot(tri, lf_hi, preferred_element_type=F32)
             + jnp.dot(tri, lf_lo, preferred_element_type=F32))
        vb = v.astype(BF16)
        state = st_ref[hh]
        o = jnp.dot((q * jnp.exp2(b)).astype(BF16), state.astype(BF16), preferred_element_type=F32)
        for s in range(SUBLANES):
            msk = (sub <= s) if rev else (sub >= s)
            e = jnp.exp2(jnp.where(msk, b - _bcast_row(b, SUBLANES, s), NEG))
            sc = jnp.sum(q * _bcast_row(k, SUBLANES, s) * e, axis=-1, keepdims=True)
            o = o + sc * _bcast_row(v, SUBLANES, s)
        scores = jnp.zeros((C, C), F32)
        m = SUBLANES
        while m < C:
            right = ((t_idx // m) % 2) == 1
            rr = _bcast_row(b, 2 * m, m if rev else m - 1)
            qside = jnp.logical_not(right) if rev else right
            kside = right if rev else jnp.logical_not(right)
            qe = jnp.where(qside, q * jnp.exp2(jnp.where(qside, b - rr, 0.0)), 0.0)
            ke = jnp.where(kside, k * jnp.exp2(jnp.where(kside, rr - b, 0.0)), 0.0)
            sl = lax.dot_general(qe.astype(BF16), ke.astype(BF16), NT, preferred_element_type=F32)
            scores = scores + jnp.where((row // (2 * m)) == (col // (2 * m)), sl, 0.0)
            m *= 2
        o = o + jnp.dot(scores.astype(BF16), vb, preferred_element_type=F32)
        o_ref[rws, cols] = o
        bl = b[0:1, :] if rev else b[C - 1:C, :]
        ke = (k * jnp.exp2(bl - b)).astype(BF16)
        upd = lax.dot_general(ke, vb, TN, preferred_element_type=F32)
        decay = jnp.transpose(jnp.broadcast_to(jnp.exp2(bl), (HG_DIM, HG_DIM)))
        st_ref[hh] = state * decay + upd

    def chunk(ci, carry):
        cc = (nchunk - 1 - ci) if rev else ci
        rws = pl.ds(pl.multiple_of(cc * C, C), C)
        for hh in range(HG_PAIR):
            one_head(rws, hh)
        return carry

    lax.fori_loop(0, nchunk, chunk, 0)


def _hgrn_direction(y, lb, bsz, seq_len, rev):
    lbk = min(HG_BLOCK, seq_len)
    assert seq_len % lbk == 0 and lbk % HG_CHUNK == 0
    nb = seq_len // lbk
    fpart = 3 if rev else 2
    npair = HG_HEADS // HG_PAIR
    width = HG_PAIR * HG_DIM

    def spec(part):
        return pl.BlockSpec((lbk, width),
                            lambda b, h, c: (b * nb + ((nb - 1 - c) if rev else c), part * npair + h))
    return pl.pallas_call(
        functools.partial(_hgrn_body, rev=rev, nchunk=lbk // HG_CHUNK),
        grid=(bsz, npair, nb),
        in_specs=[spec(0), spec(1), spec(fpart), pl.BlockSpec((1, width), lambda b, h, c: (0, h))],
        out_specs=spec(0),
        out_shape=jax.ShapeDtypeStruct((bsz * seq_len, D_MODEL), F32),
        scratch_shapes=[pltpu.VMEM((HG_PAIR, HG_DIM, HG_DIM), F32)],
        compiler_params=_cparams(3), name="hgrn_bwd" if rev else "hgrn_fwd")(y, y, y, lb.reshape(1, D_MODEL))


def _flash_body(q_ref, k_ref, v_ref, o_ref, m_sc, acc_sc):
    ki = pl.program_id(3)

    @pl.when(ki == 0)
    def _():
        m_sc[...] = jnp.full_like(m_sc, NEG)
        acc_sc[...] = jnp.zeros_like(acc_sc)

    lane = lax.broadcasted_iota(I32, (1, LANES), 1)
    lo = lane < 64
    zero = jnp.zeros((), BF16)
    k = k_ref[...]
    v1 = jnp.where(lo, v_ref[...], jnp.ones((), BF16))

    def scores(hd):
        qp = q_ref[:, (hd // 2) * LANES:(hd // 2 + 1) * LANES]
        qm = jnp.where(lo if hd % 2 == 0 else jnp.logical_not(lo), qp, zero)
        return lax.dot_general(qm, k, NT, preferred_element_type=F32)

    n_heads = GQA_HEADS // GQA_KV
    s_next = scores(0)
    for hd in range(n_heads):
        s = s_next
        if hd + 1 < n_heads:
            s_next = scores(hd + 1)
        m_prev = m_sc[hd]
        m_new = jnp.maximum(m_prev, jnp.max(s, axis=-1, keepdims=True))
        p = jnp.exp2(s - m_new)
        acc_sc[hd] = (jnp.exp2(m_prev - m_new) * acc_sc[hd]
                      + jnp.dot(p.astype(BF16), v1, preferred_element_type=F32))
        m_sc[hd] = m_new

    @pl.when(ki == pl.num_programs(3) - 1)
    def _():
        for pair in range(2):
            a0 = acc_sc[2 * pair]
            a1 = pltpu.roll(acc_sc[2 * pair + 1], 64, 1)
            o_ref[:, pair * LANES:(pair + 1) * LANES] = jnp.where(
                lo, a0 / a0[:, 64:65], a1 / a1[:, 0:1]).astype(o_ref.dtype)


def _flash_gqa(qk, v, bsz, seq_len):
    tq = min(256, seq_len)
    tk = min(8192, seq_len)
    nq, nk = seq_len // tq, seq_len // tk
    gw = (GQA_HEADS // GQA_KV) * GQA_DH
    return pl.pallas_call(
        _flash_body, grid=(bsz, GQA_KV, nq, nk),
        in_specs=[pl.BlockSpec((tq, gw), lambda b, g, i, j: (b * nq + i, g)),
                  pl.BlockSpec((tk, LANES), lambda b, g, i, j: (b * nk + j, D_MODEL // LANES + g)),
                  pl.BlockSpec((tk, LANES), lambda b, g, i, j: (b * nk + j, g))],
        out_specs=pl.BlockSpec((tq, gw), lambda b, g, i, j: (b * nq + i, g)),
        out_shape=jax.ShapeDtypeStruct((bsz * seq_len, D_MODEL), BF16),
        scratch_shapes=[pltpu.VMEM((4, tq, 1), F32), pltpu.VMEM((4, tq, LANES), F32)],
        compiler_params=_cparams(4), name="flash_gqa")(qk, qk, v)


RET_CHUNK = 128
RET_BLOCK = 512
RET_PAIR = 4


def _ret_body(q_ref, k_ref, v_ref, dm_ref, qd_ref, kd_ref, cd_ref, o_ref, st_ref, *, rev, nchunk):
    C = RET_CHUNK

    @pl.when(pl.program_id(2) == 0)
    def _():
        st_ref[...] = jnp.zeros_like(st_ref)

    def chunk(ci, carry):
        cc = (nchunk - 1 - ci) if rev else ci
        rws = pl.ds(pl.multiple_of(cc * C, C), C)
        for hh in range(RET_PAIR):
            kcols = slice(hh * RET_DK, (hh + 1) * RET_DK)
            vcols = slice(hh * RET_DV, (hh + 1) * RET_DV)
            q = q_ref[rws, kcols]
            k = k_ref[rws, kcols]
            v = v_ref[rws, vcols]
            state = st_ref[hh]
            s = lax.dot_general(q, k, NT, preferred_element_type=F32) * dm_ref[hh]
            o = jnp.dot(s.astype(BF16), v, preferred_element_type=F32)
            qs = (q.astype(F32) * qd_ref[hh]).astype(BF16)
            o = o + jnp.dot(qs, state.astype(BF16), preferred_element_type=F32)
            o_ref[rws, vcols] = o
            ks = (k.astype(F32) * kd_ref[hh]).astype(BF16)
            st_ref[hh] = state * cd_ref[hh] + lax.dot_general(ks, v, TN, preferred_element_type=F32)
        return carry

    lax.fori_loop(0, nchunk, chunk, 0)


def _ret_tables(rev):
    C = RET_CHUNK
    log_gamma = jnp.log1p(-jnp.exp2(-5.0 - jnp.arange(RET_HEADS, dtype=F32)))[:, None, None]
    pos = jnp.arange(C, dtype=F32)
    rel = pos[:, None] - pos[None, :]
    if rev:
        rel = -rel
        qpow, kpow = C - pos, pos
    else:
        qpow, kpow = pos + 1.0, C - 1.0 - pos
    dm = jnp.where(rel[None] >= 0, jnp.exp(rel[None] * log_gamma), 0.0)
    qd = jnp.broadcast_to(jnp.exp(qpow[None, :, None] * log_gamma), (RET_HEADS, C, RET_DK))
    kd = jnp.broadcast_to(jnp.exp(kpow[None, :, None] * log_gamma), (RET_HEADS, C, RET_DK))
    cd = jnp.broadcast_to(jnp.exp(C * log_gamma), (RET_HEADS, 1, RET_DV))
    return dm.astype(F32), qd.astype(F32), kd.astype(F32), cd.astype(F32)


def _ret_direction(qk, v, bsz, seq_len, rev):
    lbk = min(RET_BLOCK, seq_len)
    nb = seq_len // lbk
    npair = RET_HEADS // RET_PAIR
    rowblk = lambda b, h, c: b * nb + ((nb - 1 - c) if rev else c)
    tab = lambda shape: pl.BlockSpec((RET_PAIR,) + shape, lambda b, h, c: (h, 0, 0))
    return pl.pallas_call(
        functools.partial(_ret_body, rev=rev, nchunk=lbk // RET_CHUNK),
        grid=(bsz, npair, nb),
        in_specs=[pl.BlockSpec((lbk, RET_PAIR * RET_DK), lambda b, h, c: (rowblk(b, h, c), h)),
                  pl.BlockSpec((lbk, RET_PAIR * RET_DK), lambda b, h, c: (rowblk(b, h, c), npair + h)),
                  pl.BlockSpec((lbk, RET_PAIR * RET_DV), lambda b, h, c: (rowblk(b, h, c), h)),
                  tab((RET_CHUNK, RET_CHUNK)), tab((RET_CHUNK, RET_DK)), tab((RET_CHUNK, RET_DK)),
                  tab((1, RET_DV))],
        out_specs=pl.BlockSpec((lbk, RET_PAIR * RET_DV), lambda b, h, c: (rowblk(b, h, c), h)),
        out_shape=jax.ShapeDtypeStruct((bsz * seq_len, RET_HEADS * RET_DV), F32),
        scratch_shapes=[pltpu.VMEM((RET_PAIR, RET_DK, RET_DV), F32)],
        compiler_params=_cparams(3), name="ret_bwd" if rev else "ret_fwd")(qk, qk, v, *_ret_tables(rev))


MOE_FF_CHUNK = 512


MOE_SLOTS = 3


def _moe_body(idx_ref, nx1_ref, nx2_ref, u_hbm, gate_ref, wg_ref, wu_ref, wd_ref, hi_ref, lo_ref, xbuf, sem, *, tc):
    nsteps = pl.num_programs(0) * pl.num_programs(1)
    step = pl.program_id(0) * pl.num_programs(1) + pl.program_id(1)
    slot = step % MOE_SLOTS

    def row_copy(token, r, s):
        src = u_hbm.at[pl.ds(pl.multiple_of(token * SUBLANES, SUBLANES), SUBLANES)]
        return pltpu.make_async_copy(src, xbuf.at[s, pl.ds(r * SUBLANES, SUBLANES)], sem.at[s])

    def wait_rows(s):
        for _ in range(tc):
            row_copy(0, 0, s).wait()

    @pl.when(step == 0)
    def _():
        def issue(r, carry):
            row_copy(idx_ref[0, 0, r], r, 0).start()
            row_copy(nx1_ref[0, 0, r], r, 1).start()
            return carry
        lax.fori_loop(0, tc, issue, 0)

    wait_rows(slot)
    x = jnp.concatenate([xbuf[slot, pl.ds(s, tc, stride=SUBLANES), :] for s in range(SUBLANES)],
                        axis=1).astype(BF16)
    ahead = (step + 2) % MOE_SLOTS
    for r in range(tc):
        row_copy(nx2_ref[0, 0, r], r, ahead).start()
    acc = jnp.zeros((tc, D_MODEL), F32)
    for f in range(EXPERT_FF // MOE_FF_CHUNK):
        cols = slice(f * MOE_FF_CHUNK, (f + 1) * MOE_FF_CHUNK)
        g = jnp.dot(x, wg_ref[0, 0, :, cols], preferred_element_type=F32)
        up = jnp.dot(x, wu_ref[0, 0, :, cols], preferred_element_type=F32)
        hid = ((g * _sigmoid(g)) * up).astype(BF16)
        acc = acc + jnp.dot(hid, wd_ref[0, 0, cols, :], preferred_element_type=F32)
    gate = gate_ref[0]
    for c in range(D_MODEL // LANES):
        cols = slice(c * LANES, (c + 1) * LANES)
        ye = acc[:, cols] * gate
        hi = ye.astype(BF16)
        hi_ref[0, :, cols] = hi
        lo_ref[0, :, cols] = (ye - hi.astype(F32)).astype(BF16)

    @pl.when(step == nsteps - 1)
    def _():
        wait_rows((step + 1) % MOE_SLOTS)
        wait_rows(ahead)


def _moe_experts(u, idx_sorted, gate_sorted, wg, wu, wd, layer):
    cap = idx_sorted.shape[1]
    tc = min(512, cap)
    nj = cap // tc
    nsteps = N_EXPERTS * nj
    assert nsteps >= MOE_SLOTS
    idx3 = idx_sorted.reshape(nsteps, 1, tc)
    gate_b = jnp.broadcast_to(gate_sorted[:, :, None], (N_EXPERTS, cap, LANES))
    ispec = lambda d: pl.BlockSpec((1, 1, tc), lambda e, j: (jnp.minimum(e * nj + j + d, nsteps - 1), 0, 0),
                                   memory_space=pltpu.SMEM)
    wspec = lambda shape: pl.BlockSpec((1, 1) + shape, lambda e, j: (layer, e, 0, 0))
    ospec = pl.BlockSpec((1, tc, D_MODEL), lambda e, j: (e, j, 0))
    oshape = jax.ShapeDtypeStruct((N_EXPERTS, cap, D_MODEL), BF16)
    return pl.pallas_call(
        functools.partial(_moe_body, tc=tc), grid=(N_EXPERTS, nj),
        in_specs=[ispec(0), ispec(1), ispec(2),
                  pl.BlockSpec(memory_space=pl.ANY),
                  pl.BlockSpec((1, tc, LANES), lambda e, j: (e, j, 0)),
                  wspec((D_MODEL, EXPERT_FF)), wspec((D_MODEL, EXPERT_FF)), wspec((EXPERT_FF, D_MODEL))],
        out_specs=[ospec, ospec], out_shape=[oshape, oshape],
        scratch_shapes=[pltpu.VMEM((MOE_SLOTS, tc * SUBLANES, LANES), F32),
                        pltpu.SemaphoreType.DMA((MOE_SLOTS,))],
        compiler_params=_cparams(2), name="moe_experts")(idx3, idx3, idx3, u, gate_b, wg, wu, wd)


CMB_TOKENS = 256
CMB_WINDOW = LANES // 2
CMB_ALIGN = 2 * SUBLANES


def _combine_body(base_ref, nr_ref, h_ref, pos_ref, hi_hbm, lo_hbm, o_ref, buf, xbuf, sem, xsem, *, cap, ntiles):
    i = pl.program_id(0)
    slot = i % 2
    Wn = CMB_WINDOW
    lane = lax.broadcasted_iota(I32, (1, LANES), 1)
    lo_half = lane < Wn

    def window(tile, e, k):
        start = (base_ref[e * ntiles + tile] // CMB_ALIGN) * CMB_ALIGN + k * Wn
        w0 = pl.multiple_of(jnp.minimum(start, cap - Wn), CMB_ALIGN)
        return start, w0

    def copies(tile, k, dst, dsem):
        out = []
        for e in range(N_EXPERTS):
            w0 = window(tile, e, k)[1]
            out.append(pltpu.make_async_copy(hi_hbm.at[e, pl.ds(w0, Wn)], dst.at[0, e], dsem))
            out.append(pltpu.make_async_copy(lo_hbm.at[e, pl.ds(w0, Wn)], dst.at[1, e], dsem))
        return out

    def one_hot(k, check_start):
        parts = []
        for e in range(0, N_EXPERTS, 2):
            s0, w0 = window(i, e, k)
            s1, w1 = window(i, e + 1, k)
            p0 = pos_ref[:, e:e + 1]
            p1 = pos_ref[:, e + 1:e + 2]
            hit = jnp.where(lo_half, p0 - w0, p1 - w1 + Wn) == lane
            if check_start:
                hit = jnp.logical_and(hit, jnp.where(lo_half, p0 - s0, p1 - s1) >= 0)
            parts.append(jnp.where(hit, 1.0, 0.0).astype(BF16))
        return jnp.concatenate(parts, axis=1)

    def placed(smat, src):
        return (jnp.dot(smat, src[0].reshape(N_EXPERTS * Wn, D_MODEL), preferred_element_type=F32)
                + jnp.dot(smat, src[1].reshape(N_EXPERTS * Wn, D_MODEL), preferred_element_type=F32))

    @pl.when(i == 0)
    def _():
        for c in copies(0, 0, buf.at[0], sem.at[0]):
            c.start()

    @pl.when(i + 1 < ntiles)
    def _():
        for c in copies(i + 1, 0, buf.at[1 - slot], sem.at[1 - slot]):
            c.start()

    smat = one_hot(0, False)
    for c in copies(i, 0, buf.at[slot], sem.at[slot]):
        c.wait()
    acc = h_ref[...] + placed(smat, buf[slot])

    def extra_round(k, acc):
        for c in copies(i, k, xbuf, xsem):
            c.start()
        smat = one_hot(k, True)
        for c in copies(i, k, xbuf, xsem):
            c.wait()
        return acc + placed(smat, xbuf[...])

    o_ref[...] = lax.fori_loop(1, nr_ref[i], extra_round, acc)


def _moe_combine(h, ye_hi, ye_lo, pos_t, base, nrounds):
    T = h.shape[0]
    cap = ye_hi.shape[1]
    ntiles = T // CMB_TOKENS
    assert cap >= CMB_WINDOW and cap % CMB_ALIGN == 0
    wshape = (2, N_EXPERTS, CMB_WINDOW, D_MODEL)
    grid_spec = pltpu.PrefetchScalarGridSpec(
        num_scalar_prefetch=2, grid=(ntiles,),
        in_specs=[pl.BlockSpec((CMB_TOKENS, D_MODEL), lambda i, b, n: (i, 0)),
                  pl.BlockSpec((CMB_TOKENS, N_EXPERTS), lambda i, b, n: (i, 0)),
                  pl.BlockSpec(memory_space=pl.ANY), pl.BlockSpec(memory_space=pl.ANY)],
        out_specs=pl.BlockSpec((CMB_TOKENS, D_MODEL), lambda i, b, n: (i, 0)),
        scratch_shapes=[pltpu.VMEM((2,) + wshape, BF16), pltpu.VMEM(wshape, BF16),
                        pltpu.SemaphoreType.DMA((2,)), pltpu.SemaphoreType.DMA(())])
    return pl.pallas_call(
        functools.partial(_combine_body, cap=cap, ntiles=ntiles), grid_spec=grid_spec,
        out_shape=jax.ShapeDtypeStruct((T, D_MODEL), F32),
        compiler_params=_cparams(1), name="moe_combine")(base.reshape(-1), nrounds, h, pos_t, ye_hi, ye_lo)


SEL_ROWS = 4


def _flag(cond):
    return jnp.where(cond, 1.0, 0.0)


def _select_body(aff_ref, pos_ref, idx_ref, gate_ref, thr_sc, *, cap):
    n_exp, n_chunk, _ = aff_ref.shape

    def bit_step(i, prefix):
        cand = prefix | jnp.left_shift(jnp.int32(1), 30 - i)
        hit = _flag(pltpu.bitcast(aff_ref[...], I32) >= cand)
        cnt = jnp.sum(jnp.sum(hit, axis=1, keepdims=True), axis=2, keepdims=True)
        return jnp.where(cnt >= cap, cand, prefix)

    thr = lax.fori_loop(0, 31, bit_step, jnp.zeros((n_exp, 1, 1), I32))
    thr_sc[...] = jnp.broadcast_to(thr, thr_sc.shape)

    li = lax.broadcasted_iota(I32, (LANES, LANES), 0)
    lj = lax.broadcasted_iota(I32, (LANES, LANES), 1)
    upper = _flag(li <= lj).astype(BF16)
    ci = lax.broadcasted_iota(I32, (n_chunk, n_chunk), 0)
    cj = lax.broadcasted_iota(I32, (n_chunk, n_chunk), 1)
    before = _flag(cj < ci).astype(BF16)
    chunk_id = lax.broadcasted_iota(I32, (n_chunk, LANES), 0).astype(F32)
    lane_id = lax.broadcasted_iota(I32, (LANES, LANES), 0).astype(F32)
    slot_lane = lax.broadcasted_iota(I32, (1, LANES), 1)

    def counts(flags):
        local = jnp.dot(flags.astype(BF16), upper, preferred_element_type=F32)
        total = jnp.broadcast_to(local[:, LANES - 1:LANES], local.shape)
        return local, total, jnp.dot(before, total.astype(BF16), preferred_element_type=F32)

    def per_expert(e, carry):
        aff = aff_ref[e]
        key = pltpu.bitcast(aff, I32)
        t = thr_sc[e]
        gt = _flag(key > t)
        eq = _flag(key == t)
        need = cap - jnp.sum(jnp.sum(gt, axis=1, keepdims=True), axis=0, keepdims=True)
        eq_local, _, eq_off = counts(eq)
        sel = gt + eq * _flag(eq_local + eq_off - eq < need)
        local, total, off = counts(sel)
        pos_ref[e] = jnp.where(sel > 0.0, local + off - 1.0, -1.0).astype(I32)
        reached = off + total
        local_t = jnp.transpose(local).astype(BF16)
        aff_tr = jnp.transpose(aff)

        def one_row(r):
            slot = (r * LANES + slot_lane).astype(F32)
            chunk = jnp.sum(_flag(reached <= slot), axis=0, keepdims=True)
            skipped = jnp.sum(jnp.where(chunk_id < chunk, total, 0.0), axis=0, keepdims=True)
            pick = _flag(chunk_id == chunk)
            run = jnp.dot(local_t, pick.astype(BF16), preferred_element_type=F32)
            lane = jnp.sum(_flag(run <= slot - skipped), axis=0, keepdims=True)
            idx_ref[e, pl.ds(r, 1), :] = (chunk * LANES + lane).astype(I32)
            vals = jnp.dot(aff_tr, pick, precision=lax.Precision.HIGHEST, preferred_element_type=F32)
            gate_ref[e, pl.ds(r, 1), :] = jnp.sum(jnp.where(lane_id == lane, vals, 0.0), axis=0, keepdims=True)

        def row_group(g, c2):
            for i in range(SEL_ROWS):
                one_row(g * SEL_ROWS + i)
            return c2

        lax.fori_loop(0, cap // (LANES * SEL_ROWS), row_group, 0)
        return carry

    lax.fori_loop(0, n_exp, per_expert, 0)


def _select(aff_t, cap):
    E, T = aff_t.shape
    assert T % LANES == 0 and cap % (LANES * SEL_ROWS) == 0
    full = lambda shape: pl.BlockSpec(shape, lambda i: (0, 0, 0))
    shapes = [(E, T // LANES, LANES), (E, cap // LANES, LANES), (E, cap // LANES, LANES)]
    pos, idx, gate = pl.pallas_call(
        functools.partial(_select_body, cap=cap), grid=(1,),
        in_specs=[full(shapes[0])], out_specs=[full(s) for s in shapes],
        out_shape=[jax.ShapeDtypeStruct(shapes[0], I32), jax.ShapeDtypeStruct(shapes[1], I32),
                   jax.ShapeDtypeStruct(shapes[2], F32)],
        scratch_shapes=[pltpu.VMEM((E, 1, LANES), I32)],
        compiler_params=_cparams(1), name="moe_select")(aff_t.reshape(shapes[0]))
    return pos.reshape(E, T), idx.reshape(E, cap), gate.reshape(E, cap)


def _route(aff_t, cap):
    E, T = aff_t.shape
    ntiles = T // CMB_TOKENS
    pos, idx_s, gate_s = _select(aff_t, cap)
    cnt = jnp.sum((pos >= 0).reshape(E, ntiles, CMB_TOKENS), axis=-1, dtype=I32)
    base = jnp.cumsum(cnt, axis=1, dtype=I32) - cnt
    need = jnp.where(cnt > 0, (base % CMB_ALIGN + cnt + CMB_WINDOW - 1) // CMB_WINDOW, 0)
    return idx_s, gate_s, pos.T, base, jnp.max(need, axis=0).astype(I32)


def _moe_layer(h, u, aff_t, wg, wu, wd, layer):
    T = h.shape[0]
    cap = CAPACITY_FACTOR * T // N_EXPERTS
    idx_s, gate_s, pos_t, base, nrounds = _route(aff_t, cap)
    ye_hi, ye_lo = _moe_experts(u, idx_s, gate_s, wg, wu, wd, layer)
    return _moe_combine(h, ye_hi, ye_lo, pos_t, base, nrounds)


def _rope_tables(seq_len, n_freq, reps):
    t = jnp.arange(seq_len)
    inv_freq = ROPE_THETA ** (-jnp.arange(n_freq, dtype=F32) / n_freq)
    out = []
    for pos in ((t // GRID_W).astype(F32), (t % GRID_W).astype(F32)):
        ang = pos[:, None] * inv_freq[None, :]
        c = jnp.concatenate([jnp.cos(ang), jnp.cos(ang)], axis=1)
        s = jnp.concatenate([-jnp.sin(ang), jnp.sin(ang)], axis=1)
        out.append((c, s))
    cos = jnp.concatenate([out[0][0], out[1][0]], axis=1)
    sin = jnp.concatenate([out[0][1], out[1][1]], axis=1)
    return jnp.tile(cos, (1, reps)), jnp.tile(sin, (1, reps))


def _dup_heads(w, n_heads, dh):
    k = w.shape[0]
    return jnp.repeat(w.reshape(k, n_heads, 1, dh), 2, axis=2).reshape(k, n_heads * 2 * dh)


def _trunk(x, p):
    bsz, seq_len, _ = x.shape
    T = bsz * seq_len
    h = x.reshape(T, D_MODEL)
    for layer in range(4):
        nm = p["norm_mix"][layer]
        if layer == 0:
            qk = _proj(h, nm, p["na_wqk"], seq_len, BF16, headnorm=True, colvec=p["na_qk_gain"], name="na_proj_qk")
            v = _proj(h, nm, p["na_wv"], seq_len, BF16, name="na_proj_v")
            mix_args = (_na_attention(qk, v, p["na_bt"], bsz, seq_len),)
            kind, w_out = "plain", p["na_wo"]
        elif layer == 1:
            y = _proj(h, nm, p["hg_win"], seq_len, F32, name="hg_proj")
            o_f = _hgrn_direction(y, p["hg_lb"], bsz, seq_len, False)
            o_b = _hgrn_direction(y, p["hg_lb"], bsz, seq_len, True)
            mix_args = (o_f, o_b, y, 4, p["hg_o_gain"])
            kind, w_out = "hg", p["hg_wo"]
        elif layer == 2:
            cos, sin = _rope_tables(seq_len, 16, 2)
            qk = _proj(h, nm, p["gq_wqk"], seq_len, BF16, headnorm=True, colvec=p["gq_qk_gain"],
                       rope=(16, cos, sin), name="gqa_proj_qk")
            v = _proj(h, nm, p["gq_wv"], seq_len, BF16, name="gqa_proj_v")
            mix_args = (_flash_gqa(qk, v, bsz, seq_len),)
            kind, w_out = "plain", p["gq_wo"]
        else:
            cos, sin = _rope_tables(seq_len, 64, 1)
            qk = _proj(h, nm, p["rt_wqk"], seq_len, BF16, colvec=p["rt_qk_scale"],
                       rope=(64, cos, sin), name="ret_proj_qk")
            v = _proj(h, nm, p["rt_wv"], seq_len, BF16, name="ret_proj_v")
            g = _proj(h, nm, p["rt_wg"], seq_len, F32, name="ret_proj_g")
            o_f = _ret_direction(qk, v, bsz, seq_len, False)
            o_b = _ret_direction(qk, v, bsz, seq_len, True)
            mix_args = (o_f, o_b, g, 0, p["rt_o_gain"])
            kind, w_out = "rt", p["rt_wo"]
        h, u, aff_t = _outproj(h, mix_args, w_out, p["norm_ffn"][layer], p["router_t"][layer], kind,
                               name=f"outproj_{kind}")
        h = _moe_layer(h, u, aff_t, p["moe_wg"], p["moe_wu"], p["moe_wd"], layer)
    return h.reshape(bsz, seq_len, D_MODEL)


def kernel(x_prompt, x_sample, norm_mix, norm_ffn, na_w_in, na_q_gain, na_k_gain, na_rel_bias, na_w_out, hg_w_in, hg_lb, hg_o_gain, hg_w_out, gq_w_in, gq_q_gain, gq_k_gain, gq_w_out, rt_w_in, rt_o_gain, rt_w_out, moe_router, moe_w_gate, moe_w_up, moe_w_down):
    bf = lambda a: a.astype(BF16)
    na_w, gq_w, rt_w = na_w_in[0], gq_w_in[0], rt_w_in[0]
    lb_cum = jnp.cumsum(jax.nn.softmax(hg_lb.astype(F32), axis=0), axis=0)
    qd = GQA_HEADS * GQA_DH
    kd = GQA_KV * GQA_DH
    rq = RET_HEADS * RET_DK
    rv = RET_HEADS * RET_DV
    p = {
        "norm_mix": norm_mix, "norm_ffn": norm_ffn,
        "na_wqk": bf(na_w[:, :2 * D_MODEL]), "na_wv": bf(na_w[:, 2 * D_MODEL:]),
        "na_qk_gain": jnp.concatenate([jnp.tile(na_q_gain[0], NA_HEADS) * (64 ** -0.5 * LOG2E),
                                       jnp.tile(na_k_gain[0], NA_HEADS)]),
        "na_bt": _na_bias_table(na_rel_bias[0]), "na_wo": bf(na_w_out[0]),
        "hg_win": bf(hg_w_in[0]), "hg_lb": lb_cum[1] - lb_cum[0], "hg_o_gain": hg_o_gain[0],
        "hg_wo": bf(hg_w_out[0]),
        "gq_wqk": bf(jnp.concatenate([gq_w[:, :qd], _dup_heads(gq_w[:, qd:qd + kd], GQA_KV, GQA_DH)], axis=1)),
        "gq_wv": bf(_dup_heads(gq_w[:, qd + kd:], GQA_KV, GQA_DH)),
        "gq_qk_gain": jnp.concatenate([jnp.tile(gq_q_gain[0], GQA_HEADS) * (GQA_DH ** -0.5 * LOG2E),
                                       jnp.tile(gq_k_gain[0], 2 * GQA_KV)]),
        "gq_wo": bf(gq_w_out[0]),
        "rt_wqk": bf(rt_w[:, :2 * rq]), "rt_wv": bf(rt_w[:, 2 * rq:2 * rq + rv]), "rt_wg": bf(rt_w[:, 2 * rq + rv:]),
        "rt_qk_scale": jnp.concatenate([jnp.full((rq,), RET_DK ** -0.5, F32), jnp.ones((rq,), F32)]),
        "rt_o_gain": rt_o_gain[0], "rt_wo": bf(rt_w_out[0]),
        "router_t": jnp.swapaxes(moe_router, 1, 2).astype(F32),
        "moe_wg": bf(moe_w_gate), "moe_wu": bf(moe_w_up), "moe_wd": bf(moe_w_down),
    }
    return (_trunk(x_prompt, p), _trunk(x_sample, p))
```

```python
import functools
import math

import jax
import jax.numpy as jnp
from jax import lax
from jax.experimental import pallas as pl
from jax.experimental.pallas import tpu as pltpu

F32 = jnp.float32
BF16 = jnp.bfloat16
I32 = jnp.int32

D_MODEL = 1024
GRID_W = 64
EPS = 1e-6
ROPE_THETA = 10000.0
NEG = -1e30
LOG2E = math.log2(math.e)

NA_HEADS = 16
NA_WIN_ROWS = 8
NA_WIN_COLS = 16
HG_HEADS = 8
HG_DIM = 128
GQA_HEADS = 16
GQA_KV = 4
GQA_DH = 64
RET_HEADS = 4
RET_DK = 256
RET_DV = 512
N_EXPERTS = 16
EXPERT_FF = 2048
CAPACITY_FACTOR = 2

LANES = 128
SUBLANES = 8
VMEM_LIMIT = 52 * 1024 * 1024

NT = (((1,), (1,)), ((), ()))
TN = (((0,), (0,)), ((), ()))


def _cparams(n_axes):
    return pltpu.CompilerParams(dimension_semantics=("arbitrary",) * n_axes,
                                vmem_limit_bytes=VMEM_LIMIT)


def _sigmoid(x):
    return 1.0 / (1.0 + jnp.exp(-x))


def _proj_body(*refs, headnorm, has_cv, rope_k, tw):
    it = iter(refs)
    x_ref, g_ref, w_ref = next(it), next(it), next(it)
    cv_ref = next(it) if has_cv else None
    cos_ref = next(it) if rope_k else None
    sin_ref = next(it) if rope_k else None
    o_ref, xn_ref = next(it), next(it)

    @pl.when(pl.program_id(1) == 0)
    def _():
        x = x_ref[...]
        ms = jnp.mean(x * x, axis=-1, keepdims=True)
        xn_ref[...] = (x * lax.rsqrt(ms + EPS) * g_ref[...]).astype(BF16)

    acc = jnp.dot(xn_ref[...], w_ref[pl.program_id(1)], preferred_element_type=F32)
    if not (headnorm or has_cv or rope_k):
        o_ref[...] = acc.astype(o_ref.dtype)
        return
    lane = lax.broadcasted_iota(I32, (1, LANES), 1)
    if headnorm:
        same_head = _flag((lax.broadcasted_iota(I32, (LANES, LANES), 0) < 64)
                          == (lax.broadcasted_iota(I32, (LANES, LANES), 1) < 64)).astype(BF16)
    for c in range(acc.shape[1] // LANES):
        cols = slice(c * LANES, (c + 1) * LANES)
        a = acc[:, cols]
        if headnorm:
            sq = a * a
            sq_hi = sq.astype(BF16)
            sq_lo = (sq - sq_hi.astype(F32)).astype(BF16)
            ssq = (jnp.dot(sq_hi, same_head, preferred_element_type=F32)
                   + jnp.dot(sq_lo, same_head, preferred_element_type=F32))
            a = a * lax.rsqrt(ssq * (1.0 / 64.0) + EPS)
        if has_cv:
            a = a * cv_ref[:, cols]
        if rope_k:
            tcol = (c * LANES) % tw
            if rope_k == 64:
                partner = pltpu.roll(a, 64, 1)
            else:
                partner = jnp.where((lane & rope_k) != 0, pltpu.roll(a, rope_k, 1),
                                    pltpu.roll(a, LANES - rope_k, 1))
            a = a * cos_ref[:, tcol:tcol + LANES] + partner * sin_ref[:, tcol:tcol + LANES]
        o_ref[:, cols] = a.astype(o_ref.dtype)


def _proj(x, g, w, seq_len, out_dtype, *, headnorm=False, colvec=None, rope=None, name="proj"):
    T = x.shape[0]
    N = w.shape[1]
    tm = min(1024, seq_len)
    tn = 512 if N % 512 == 0 else 256
    assert T % tm == 0 and seq_len % tm == 0 and N % tn == 0
    nsb = seq_len // tm
    in_specs = [pl.BlockSpec((tm, D_MODEL), lambda i, j: (i, 0)),
                pl.BlockSpec((1, D_MODEL), lambda i, j: (0, 0)),
                pl.BlockSpec((N // tn, D_MODEL, tn), lambda i, j: (0, 0, 0))]
    w_tiles = jnp.swapaxes(w.reshape(D_MODEL, N // tn, tn), 0, 1)
    args = [x, g.reshape(1, D_MODEL).astype(F32), w_tiles]
    if colvec is not None:
        in_specs.append(pl.BlockSpec((1, tn), lambda i, j: (0, j)))
        args.append(colvec.reshape(1, N).astype(F32))
    rope_k, tw = 0, LANES
    if rope is not None:
        rope_k, cos, sin = rope
        tw = cos.shape[1]
        assert tn % tw == 0
        in_specs += [pl.BlockSpec((tm, tw), lambda i, j: (i % nsb, 0))] * 2
        args += [cos, sin]
    body = functools.partial(_proj_body, headnorm=headnorm, has_cv=colvec is not None,
                             rope_k=rope_k, tw=tw)
    return pl.pallas_call(
        body, grid=(T // tm, N // tn), in_specs=in_specs,
        out_specs=pl.BlockSpec((tm, tn), lambda i, j: (i, j)),
        out_shape=jax.ShapeDtypeStruct((T, N), out_dtype),
        scratch_shapes=[pltpu.VMEM((tm, D_MODEL), BF16)],
        compiler_params=_cparams(2), name=name)(*args)


def _group_rms(o, gain_ref, width):
    parts = []
    for c in range(o.shape[1] // width):
        a = o[:, c * width:(c + 1) * width]
        ms = jnp.mean(a * a, axis=-1, keepdims=True)
        parts.append(a * lax.rsqrt(ms + EPS) * gain_ref[...])
    return jnp.concatenate(parts, axis=1)


OUTPROJ_SPLIT = 2


def _outproj_body(*refs, kind):
    it = iter(refs)
    h_ref = next(it)
    if kind == "plain":
        a_ref = next(it)
    else:
        of_ref, ob_ref, g_ref, gain_ref = next(it), next(it), next(it), next(it)
    w_ref, nf_ref, wr_ref = next(it), next(it), next(it)
    hn_ref, u_ref, aff_ref = next(it), next(it), next(it)
    hm = h_ref.shape[0] // OUTPROJ_SPLIT
    for part in range(OUTPROJ_SPLIT):
        rows = slice(part * hm, (part + 1) * hm)
        if kind == "plain":
            a = a_ref[rows, :]
        else:
            o = of_ref[rows, :] + ob_ref[rows, :]
            g = g_ref[rows, :]
            if kind == "hg":
                a = (_group_rms(o, gain_ref, HG_DIM) * _sigmoid(g)).astype(BF16)
            else:
                a = ((g * _sigmoid(g)) * _group_rms(o, gain_ref, RET_DV)).astype(BF16)
        hn = h_ref[rows, :] + jnp.dot(a, w_ref[...], preferred_element_type=F32)
        hn_ref[rows, :] = hn
        ms = jnp.mean(hn * hn, axis=-1, keepdims=True)
        u = hn * lax.rsqrt(ms + EPS) * nf_ref[...]
        for s in range(SUBLANES):
            u_ref[pl.ds(part * hm * SUBLANES + s, hm, stride=SUBLANES), :] = u[:, s * LANES:(s + 1) * LANES]
        logits = lax.dot_general(wr_ref[...], u, NT, precision=lax.Precision.HIGHEST,
                                 preferred_element_type=F32)
        m = jnp.max(logits, axis=0, keepdims=True)
        e = jnp.exp(logits - m)
        aff_ref[:, rows] = e / jnp.sum(e, axis=0, keepdims=True)


def _outproj(h, mix_args, w_out, norm_g, w_router_t, kind, name):
    T = h.shape[0]
    tm = 256 if kind == "rt" else 512
    K = w_out.shape[0]
    row = lambda i: (i, 0)
    in_specs = [pl.BlockSpec((tm, D_MODEL), row)]
    args = [h]
    if kind == "plain":
        (a,) = mix_args
        in_specs.append(pl.BlockSpec((tm, K), row))
        args.append(a)
    else:
        o_f, o_b, g_arr, g_col, gain = mix_args
        gw = gain.shape[0]
        in_specs += [pl.BlockSpec((tm, K), row), pl.BlockSpec((tm, K), row),
                     pl.BlockSpec((tm, K), lambda i: (i, g_col)),
                     pl.BlockSpec((1, gw), lambda i: (0, 0))]
        args += [o_f, o_b, g_arr, gain.reshape(1, gw).astype(F32)]
    in_specs += [pl.BlockSpec((K, D_MODEL), lambda i: (0, 0)),
                 pl.BlockSpec((1, D_MODEL), lambda i: (0, 0)),
                 pl.BlockSpec((N_EXPERTS, D_MODEL), lambda i: (0, 0))]
    args += [w_out, norm_g.reshape(1, D_MODEL).astype(F32), w_router_t]
    return pl.pallas_call(
        functools.partial(_outproj_body, kind=kind), grid=(T // tm,), in_specs=in_specs,
        out_specs=[pl.BlockSpec((tm, D_MODEL), row), pl.BlockSpec((tm * SUBLANES, LANES), row),
                   pl.BlockSpec((N_EXPERTS, tm), lambda i: (0, i))],
        out_shape=[jax.ShapeDtypeStruct((T, D_MODEL), F32), jax.ShapeDtypeStruct((T * SUBLANES, LANES), F32),
                   jax.ShapeDtypeStruct((N_EXPERTS, T), F32)],
        compiler_params=_cparams(1), name=name)(*args)


NA_QROWS = 8
NA_GROUP = 2


def _na_body(q_ref, kp_ref, kc_ref, kn_ref, vp_ref, vc_ref, vn_ref, bt_ref, o_ref, kbuf, vbuf, *, rows):
    rb = pl.program_id(2)
    blk = NA_QROWS * GRID_W
    for n, (kr, vr) in enumerate(((kp_ref, vp_ref), (kc_ref, vc_ref), (kn_ref, vn_ref))):
        kbuf[n * blk:(n + 1) * blk, :] = kr[...]
        vbuf[n * blk:(n + 1) * blk, :] = vr[...]
    lane = lax.broadcasted_iota(I32, (1, LANES), 1)
    lo = lane < 64
    zero = jnp.zeros((), BF16)

    def scores(unit):
        qi, gi = unit
        cols = slice(gi * LANES, (gi + 1) * LANES)
        r = rb * NA_QROWS + qi
        r_start = jnp.clip(r - NA_WIN_ROWS // 2, 0, rows - NA_WIN_ROWS)
        roff0 = r_start - r + (NA_WIN_ROWS - 1)
        off = pl.multiple_of((r_start - rb * NA_QROWS + NA_QROWS) * GRID_W, GRID_W)
        kw = kbuf[pl.ds(off, NA_WIN_ROWS * GRID_W), cols]
        qp = q_ref[qi * GRID_W:(qi + 1) * GRID_W, cols]
        q2 = jnp.concatenate([jnp.where(lo, qp, zero), jnp.where(lo, zero, qp)], axis=0)
        return lax.dot_general(q2, kw, NT, preferred_element_type=F32) + bt_ref[gi, roff0], off

    units = [(qi, gi) for qi in range(NA_QROWS) for gi in range(NA_GROUP)]
    nxt = scores(units[0])
    for n, (qi, gi) in enumerate(units):
        s, off = nxt
        if n + 1 < len(units):
            nxt = scores(units[n + 1])
        cols = slice(gi * LANES, (gi + 1) * LANES)
        p = jnp.exp2(s - jnp.max(s, axis=-1, keepdims=True))
        l = jnp.sum(p, axis=-1, keepdims=True)
        o = jnp.dot(p.astype(BF16), vbuf[pl.ds(off, NA_WIN_ROWS * GRID_W), cols], preferred_element_type=F32) / l
        o_ref[qi * GRID_W:(qi + 1) * GRID_W, cols] = jnp.where(lo, o[:GRID_W], o[GRID_W:]).astype(o_ref.dtype)


def _na_bias_table(rel_bias):
    c = jnp.arange(GRID_W)
    c_start = jnp.clip(c - NA_WIN_COLS // 2, 0, GRID_W - NA_WIN_COLS)
    kc = jnp.arange(GRID_W)
    valid = (kc[None, :] >= c_start[:, None]) & (kc[None, :] < c_start[:, None] + NA_WIN_COLS)
    coff = jnp.clip(kc[None, :] - c[:, None] + (NA_WIN_COLS - 1), 0, 2 * NA_WIN_COLS - 2)
    onehot = jnp.logical_and(coff[None] == jnp.arange(2 * NA_WIN_COLS - 1)[:, None, None], valid[None])
    bm = jnp.einsum("hrj,jck->hrck", rel_bias * LOG2E, onehot.astype(F32), precision=lax.Precision.HIGHEST)
    bm = bm + jnp.where(valid, 0.0, NEG)[None, None]
    bt = jnp.stack([jnp.concatenate([bm[:, r0 + i] for i in range(NA_WIN_ROWS)], axis=-1)
                    for r0 in range(NA_WIN_ROWS)], axis=1).astype(F32)
    bt = bt.reshape(NA_HEADS // 2, 2, NA_WIN_ROWS, GRID_W, NA_WIN_ROWS * GRID_W)
    return jnp.swapaxes(bt, 1, 2).reshape(NA_HEADS // 2, NA_WIN_ROWS, 2 * GRID_W, NA_WIN_ROWS * GRID_W)


def _na_attention(qk, v, bt, bsz, seq_len):
    rows = seq_len // GRID_W
    assert rows % NA_QROWS == 0 and rows >= NA_WIN_ROWS
    nrb = rows // NA_QROWS
    blk = NA_QROWS * GRID_W
    ngrp = NA_HEADS // (2 * NA_GROUP)
    width = NA_GROUP * LANES

    def spec(col0, shift):
        return pl.BlockSpec((blk, width),
                            lambda hp, b, rb: (b * nrb + jnp.clip(rb + shift, 0, nrb - 1), col0 + hp))
    in_specs = [spec(0, 0), spec(ngrp, -1), spec(ngrp, 0), spec(ngrp, 1),
                spec(0, -1), spec(0, 0), spec(0, 1),
                pl.BlockSpec((NA_GROUP, NA_WIN_ROWS, 2 * GRID_W, NA_WIN_ROWS * GRID_W),
                             lambda hp, b, rb: (hp, 0, 0, 0))]
    return pl.pallas_call(
        functools.partial(_na_body, rows=rows), grid=(ngrp, bsz, nrb), in_specs=in_specs,
        out_specs=spec(0, 0),
        out_shape=jax.ShapeDtypeStruct((bsz * seq_len, D_MODEL), BF16),
        scratch_shapes=[pltpu.VMEM((3 * blk, width), BF16), pltpu.VMEM((3 * blk, width), BF16)],
        compiler_params=_cparams(3), name="na_attention")(qk, qk, qk, qk, v, v, v, bt)


HG_CHUNK = 128
HG_BLOCK = 512
HG_PAIR = 4


def _bcast_row(x, group, r):
    C = x.shape[0]
    x3 = x.reshape(C // group, group, x.shape[1])
    return jnp.broadcast_to(x3[:, r:r + 1, :], x3.shape).reshape(x.shape)


def _hgrn_body(xq_ref, xi_ref, xf_ref, lb_ref, o_ref, st_ref, *, rev, nchunk):
    C = HG_CHUNK

    @pl.when(pl.program_id(2) == 0)
    def _():
        st_ref[...] = jnp.zeros_like(st_ref)

    row = lax.broadcasted_iota(I32, (C, C), 0)
    col = lax.broadcasted_iota(I32, (C, C), 1)
    tri = jnp.where((row <= col) if rev else (row >= col), 1.0, 0.0).astype(BF16)
    t_idx = lax.broadcasted_iota(I32, (C, HG_DIM), 0)
    sub = t_idx % SUBLANES

    def one_head(rws, hh):
        cols = slice(hh * HG_DIM, (hh + 1) * HG_DIM)
        lbv = lb_ref[:, cols]
        q = xq_ref[rws, cols] * (HG_DIM ** -0.5)
        xi = xi_ref[rws, cols]
        v = xi * _sigmoid(xi)
        f = lbv + (1.0 - lbv) * _sigmoid(xf_ref[rws, cols])
        k = 1.0 - f
        lf = jnp.log(f) * LOG2E
        lf_hi = lf.astype(BF16)
        lf_lo = (lf - lf_hi.astype(F32)).astype(BF16)
        b = (jnp.dot(tri, lf_hi, preferred_element_type=F32)
             + jnp.dot(tri, lf_lo, preferred_element_type=F32))
        vb = v.astype(BF16)
        state = st_ref[hh]
        o = jnp.dot((q * jnp.exp2(b)).astype(BF16), state.astype(BF16), preferred_element_type=F32)
        for s in range(SUBLANES):
            msk = (sub <= s) if rev else (sub >= s)
            e = jnp.exp2(jnp.where(msk, b - _bcast_row(b, SUBLANES, s), NEG))
            sc = jnp.sum(q * _bcast_row(k, SUBLANES, s) * e, axis=-1, keepdims=True)
            o = o + sc * _bcast_row(v, SUBLANES, s)
        scores = jnp.zeros((C, C), F32)
        m = SUBLANES
        while m < C:
            right = ((t_idx // m) % 2) == 1
            rr = _bcast_row(b, 2 * m, m if rev else m - 1)
            qside = jnp.logical_not(right) if rev else right
            kside = right if rev else jnp.logical_not(right)
            qe = jnp.where(qside, q * jnp.exp2(jnp.where(qside, b - rr, 0.0)), 0.0)
            ke = jnp.where(kside, k * jnp.exp2(jnp.where(kside, rr - b, 0.0)), 0.0)
            sl = lax.dot_general(qe.astype(BF16), ke.astype(BF16), NT, preferred_element_type=F32)
            scores = scores + jnp.where((row // (2 * m)) == (col // (2 * m)), sl, 0.0)
            m *= 2
        o = o + jnp.dot(scores.astype(BF16), vb, preferred_element_type=F32)
        o_ref[rws, cols] = o
        bl = b[0:1, :] if rev else b[C - 1:C, :]
        ke = (k * jnp.exp2(bl - b)).astype(BF16)
        upd = lax.dot_general(ke, vb, TN, preferred_element_type=F32)
        decay = jnp.transpose(jnp.broadcast_to(jnp.exp2(bl), (HG_DIM, HG_DIM)))
        st_ref[hh] = state * decay + upd

    def chunk(ci, carry):
        cc = (nchunk - 1 - ci) if rev else ci
        rws = pl.ds(pl.multiple_of(cc * C, C), C)
        for hh in range(HG_PAIR):
            one_head(rws, hh)
        return carry

    lax.fori_loop(0, nchunk, chunk, 0)


def _hgrn_direction(y, lb, bsz, seq_len, rev):
    lbk = min(HG_BLOCK, seq_len)
    assert seq_len % lbk == 0 and lbk % HG_CHUNK == 0
    nb = seq_len // lbk
    fpart = 3 if rev else 2
    npair = HG_HEADS // HG_PAIR
    width = HG_PAIR * HG_DIM

    def spec(part):
        return pl.BlockSpec((lbk, width),
                            lambda b, h, c: (b * nb + ((nb - 1 - c) if rev else c), part * npair + h))
    return pl.pallas_call(
        functools.partial(_hgrn_body, rev=rev, nchunk=lbk // HG_CHUNK),
        grid=(bsz, npair, nb),
        in_specs=[spec(0), spec(1), spec(fpart), pl.BlockSpec((1, width), lambda b, h, c: (0, h))],
        out_specs=spec(0),
        out_shape=jax.ShapeDtypeStruct((bsz * seq_len, D_MODEL), F32),
        scratch_shapes=[pltpu.VMEM((HG_PAIR, HG_DIM, HG_DIM), F32)],
        compiler_params=_cparams(3), name="hgrn_bwd" if rev else "hgrn_fwd")(y, y, y, lb.reshape(1, D_MODEL))


def _flash_body(q_ref, k_ref, v_ref, o_ref, m_sc, acc_sc):
    ki = pl.program_id(3)

    @pl.when(ki == 0)
    def _():
        m_sc[...] = jnp.full_like(m_sc, NEG)
        acc_sc[...] = jnp.zeros_like(acc_sc)

    lane = lax.broadcasted_iota(I32, (1, LANES), 1)
    lo = lane < 64
    zero = jnp.zeros((), BF16)
    k = k_ref[...]
    v1 = jnp.where(lo, v_ref[...], jnp.ones((), BF16))

    def scores(hd):
        qp = q_ref[:, (hd // 2) * LANES:(hd // 2 + 1) * LANES]
        qm = jnp.where(lo if hd % 2 == 0 else jnp.logical_not(lo), qp, zero)
        return lax.dot_general(qm, k, NT, preferred_element_type=F32)

    n_heads = GQA_HEADS // GQA_KV
    s_next = scores(0)
    for hd in range(n_heads):
        s = s_next
        if hd + 1 < n_heads:
            s_next = scores(hd + 1)
        m_prev = m_sc[hd]
        m_new = jnp.maximum(m_prev, jnp.max(s, axis=-1, keepdims=True))
        p = jnp.exp2(s - m_new)
        acc_sc[hd] = (jnp.exp2(m_prev - m_new) * acc_sc[hd]
                      + jnp.dot(p.astype(BF16), v1, preferred_element_type=F32))
        m_sc[hd] = m_new

    @pl.when(ki == pl.num_programs(3) - 1)
    def _():
        for pair in range(2):
            a0 = acc_sc[2 * pair]
            a1 = pltpu.roll(acc_sc[2 * pair + 1], 64, 1)
            o_ref[:, pair * LANES:(pair + 1) * LANES] = jnp.where(
                lo, a0 / a0[:, 64:65], a1 / a1[:, 0:1]).astype(o_ref.dtype)


def _flash_gqa(qk, v, bsz, seq_len):
    tq = min(256, seq_len)
    tk = min(8192, seq_len)
    nq, nk = seq_len // tq, seq_len // tk
    gw = (GQA_HEADS // GQA_KV) * GQA_DH
    return pl.pallas_call(
        _flash_body, grid=(bsz, GQA_KV, nq, nk),
        in_specs=[pl.BlockSpec((tq, gw), lambda b, g, i, j: (b * nq + i, g)),
                  pl.BlockSpec((tk, LANES), lambda b, g, i, j: (b * nk + j, D_MODEL // LANES + g)),
                  pl.BlockSpec((tk, LANES), lambda b, g, i, j: (b * nk + j, g))],
        out_specs=pl.BlockSpec((tq, gw), lambda b, g, i, j: (b * nq + i, g)),
        out_shape=jax.ShapeDtypeStruct((bsz * seq_len, D_MODEL), BF16),
        scratch_shapes=[pltpu.VMEM((4, tq, 1), F32), pltpu.VMEM((4, tq, LANES), F32)],
        compiler_params=_cparams(4), name="flash_gqa")(qk, qk, v)


RET_CHUNK = 128
RET_BLOCK = 512
RET_PAIR = 4


def _ret_body(q_ref, k_ref, v_ref, dm_ref, qd_ref, kd_ref, cd_ref, o_ref, st_ref, *, rev, nchunk):
    C = RET_CHUNK

    @pl.when(pl.program_id(2) == 0)
    def _():
        st_ref[...] = jnp.zeros_like(st_ref)

    def chunk(ci, carry):
        cc = (nchunk - 1 - ci) if rev else ci
        rws = pl.ds(pl.multiple_of(cc * C, C), C)
        for hh in range(RET_PAIR):
            kcols = slice(hh * RET_DK, (hh + 1) * RET_DK)
            vcols = slice(hh * RET_DV, (hh + 1) * RET_DV)
            q = q_ref[rws, kcols]
            k = k_ref[rws, kcols]
            v = v_ref[rws, vcols]
            state = st_ref[hh]
            s = lax.dot_general(q, k, NT, preferred_element_type=F32) * dm_ref[hh]
            o = jnp.dot(s.astype(BF16), v, preferred_element_type=F32)
            qs = (q.astype(F32) * qd_ref[hh]).astype(BF16)
            o = o + jnp.dot(qs, state.astype(BF16), preferred_element_type=F32)
            o_ref[rws, vcols] = o
            ks = (k.astype(F32) * kd_ref[hh]).astype(BF16)
            st_ref[hh] = state * cd_ref[hh] + lax.dot_general(ks, v, TN, preferred_element_type=F32)
        return carry

    lax.fori_loop(0, nchunk, chunk, 0)


def _ret_tables(rev):
    C = RET_CHUNK
    log_gamma = jnp.log1p(-jnp.exp2(-5.0 - jnp.arange(RET_HEADS, dtype=F32)))[:, None, None]
    pos = jnp.arange(C, dtype=F32)
    rel = pos[:, None] - pos[None, :]
    if rev:
        rel = -rel
        qpow, kpow = C - pos, pos
    else:
        qpow, kpow = pos + 1.0, C - 1.0 - pos
    dm = jnp.where(rel[None] >= 0, jnp.exp(rel[None] * log_gamma), 0.0)
    qd = jnp.broadcast_to(jnp.exp(qpow[None, :, None] * log_gamma), (RET_HEADS, C, RET_DK))
    kd = jnp.broadcast_to(jnp.exp(kpow[None, :, None] * log_gamma), (RET_HEADS, C, RET_DK))
    cd = jnp.broadcast_to(jnp.exp(C * log_gamma), (RET_HEADS, 1, RET_DV))
    return dm.astype(F32), qd.astype(F32), kd.astype(F32), cd.astype(F32)


def _ret_direction(qk, v, bsz, seq_len, rev):
    lbk = min(RET_BLOCK, seq_len)
    nb = seq_len // lbk
    npair = RET_HEADS // RET_PAIR
    rowblk = lambda b, h, c: b * nb + ((nb - 1 - c) if rev else c)
    tab = lambda shape: pl.BlockSpec((RET_PAIR,) + shape, lambda b, h, c: (h, 0, 0))
    return pl.pallas_call(
        functools.partial(_ret_body, rev=rev, nchunk=lbk // RET_CHUNK),
        grid=(bsz, npair, nb),
        in_specs=[pl.BlockSpec((lbk, RET_PAIR * RET_DK), lambda b, h, c: (rowblk(b, h, c), h)),
                  pl.BlockSpec((lbk, RET_PAIR * RET_DK), lambda b, h, c: (rowblk(b, h, c), npair + h)),
                  pl.BlockSpec((lbk, RET_PAIR * RET_DV), lambda b, h, c: (rowblk(b, h, c), h)),
                  tab((RET_CHUNK, RET_CHUNK)), tab((RET_CHUNK, RET_DK)), tab((RET_CHUNK, RET_DK)),
                  tab((1, RET_DV))],
        out_specs=pl.BlockSpec((lbk, RET_PAIR * RET_DV), lambda b, h, c: (rowblk(b, h, c), h)),
        out_shape=jax.ShapeDtypeStruct((bsz * seq_len, RET_HEADS * RET_DV), F32),
        scratch_shapes=[pltpu.VMEM((RET_PAIR, RET_DK, RET_DV), F32)],
        compiler_params=_cparams(3), name="ret_bwd" if rev else "ret_fwd")(qk, qk, v, *_ret_tables(rev))


MOE_FF_CHUNK = 512


MOE_SLOTS = 3


def _moe_body(idx_ref, nx1_ref, nx2_ref, u_hbm, gate_ref, wg_ref, wu_ref, wd_ref, hi_ref, lo_ref, xbuf, sem, *, tc):
    nsteps = pl.num_programs(0) * pl.num_programs(1)
    step = pl.program_id(0) * pl.num_programs(1) + pl.program_id(1)
    slot = step % MOE_SLOTS

    def row_copy(token, r, s):
        src = u_hbm.at[pl.ds(pl.multiple_of(token * SUBLANES, SUBLANES), SUBLANES)]
        return pltpu.make_async_copy(src, xbuf.at[s, pl.ds(r * SUBLANES, SUBLANES)], sem.at[s])

    def wait_rows(s):
        for _ in range(tc):
            row_copy(0, 0, s).wait()

    @pl.when(step == 0)
    def _():
        def issue(r, carry):
            row_copy(idx_ref[0, 0, r], r, 0).start()
            row_copy(nx1_ref[0, 0, r], r, 1).start()
            return carry
        lax.fori_loop(0, tc, issue, 0)

    wait_rows(slot)
    x = jnp.concatenate([xbuf[slot, pl.ds(s, tc, stride=SUBLANES), :] for s in range(SUBLANES)],
                        axis=1).astype(BF16)
    ahead = (step + 2) % MOE_SLOTS
    for r in range(tc):
        row_copy(nx2_ref[0, 0, r], r, ahead).start(priority=r % 2)
    acc = jnp.zeros((tc, D_MODEL), F32)
    for f in range(EXPERT_FF // MOE_FF_CHUNK):
        cols = slice(f * MOE_FF_CHUNK, (f + 1) * MOE_FF_CHUNK)
        g = jnp.dot(x, wg_ref[0, 0, :, cols], preferred_element_type=F32)
        up = jnp.dot(x, wu_ref[0, 0, :, cols], preferred_element_type=F32)
        hid = ((g * _sigmoid(g)) * up).astype(BF16)
        acc = acc + jnp.dot(hid, wd_ref[0, 0, cols, :], preferred_element_type=F32)
    gate = gate_ref[0]
    for c in range(D_MODEL // LANES):
        cols = slice(c * LANES, (c + 1) * LANES)
        ye = acc[:, cols] * gate
        hi = ye.astype(BF16)
        hi_ref[0, :, cols] = hi
        lo_ref[0, :, cols] = (ye - hi.astype(F32)).astype(BF16)

    @pl.when(step == nsteps - 1)
    def _():
        wait_rows((step + 1) % MOE_SLOTS)
        wait_rows(ahead)


def _moe_experts(u, idx_sorted, gate_sorted, wg, wu, wd, layer):
    cap = idx_sorted.shape[1]
    tc = min(512, cap)
    nj = cap // tc
    nsteps = N_EXPERTS * nj
    assert nsteps >= MOE_SLOTS
    idx3 = idx_sorted.reshape(nsteps, 1, tc)
    gate_b = jnp.broadcast_to(gate_sorted[:, :, None], (N_EXPERTS, cap, LANES))
    ispec = lambda d: pl.BlockSpec((1, 1, tc), lambda e, j: (jnp.minimum(e * nj + j + d, nsteps - 1), 0, 0),
                                   memory_space=pltpu.SMEM)
    wspec = lambda shape: pl.BlockSpec((1, 1) + shape, lambda e, j: (layer, e, 0, 0))
    ospec = pl.BlockSpec((1, tc, D_MODEL), lambda e, j: (e, j, 0))
    oshape = jax.ShapeDtypeStruct((N_EXPERTS, cap, D_MODEL), BF16)
    return pl.pallas_call(
        functools.partial(_moe_body, tc=tc), grid=(N_EXPERTS, nj),
        in_specs=[ispec(0), ispec(1), ispec(2),
                  pl.BlockSpec(memory_space=pl.ANY),
                  pl.BlockSpec((1, tc, LANES), lambda e, j: (e, j, 0)),
                  wspec((D_MODEL, EXPERT_FF)), wspec((D_MODEL, EXPERT_FF)), wspec((EXPERT_FF, D_MODEL))],
        out_specs=[ospec, ospec], out_shape=[oshape, oshape],
        scratch_shapes=[pltpu.VMEM((MOE_SLOTS, tc * SUBLANES, LANES), F32),
                        pltpu.SemaphoreType.DMA((MOE_SLOTS,))],
        compiler_params=_cparams(2), name="moe_experts")(idx3, idx3, idx3, u, gate_b, wg, wu, wd)


CMB_TOKENS = 256
CMB_WINDOW = LANES // 2
CMB_ALIGN = 2 * SUBLANES


def _combine_body(base_ref, nr_ref, h_ref, pos_ref, hi_hbm, lo_hbm, o_ref, buf, xbuf, sem, xsem, *, cap, ntiles):
    i = pl.program_id(0)
    slot = i % 2
    Wn = CMB_WINDOW
    lane = lax.broadcasted_iota(I32, (1, LANES), 1)
    lo_half = lane < Wn

    def window(tile, e, k):
        start = (base_ref[e * ntiles + tile] // CMB_ALIGN) * CMB_ALIGN + k * Wn
        w0 = pl.multiple_of(jnp.minimum(start, cap - Wn), CMB_ALIGN)
        return start, w0

    def copies(tile, k, dst, dsem):
        out = []
        for e in range(N_EXPERTS):
            w0 = window(tile, e, k)[1]
            out.append(pltpu.make_async_copy(hi_hbm.at[e, pl.ds(w0, Wn)], dst.at[0, e], dsem))
            out.append(pltpu.make_async_copy(lo_hbm.at[e, pl.ds(w0, Wn)], dst.at[1, e], dsem))
        return out

    def one_hot(k, check_start):
        parts = []
        for e in range(0, N_EXPERTS, 2):
            s0, w0 = window(i, e, k)
            s1, w1 = window(i, e + 1, k)
            p0 = pos_ref[:, e:e + 1]
            p1 = pos_ref[:, e + 1:e + 2]
            hit = jnp.where(lo_half, p0 - w0, p1 - w1 + Wn) == lane
            if check_start:
                hit = jnp.logical_and(hit, jnp.where(lo_half, p0 - s0, p1 - s1) >= 0)
            parts.append(jnp.where(hit, 1.0, 0.0).astype(BF16))
        return jnp.concatenate(parts, axis=1)

    def placed(smat, src):
        return (jnp.dot(smat, src[0].reshape(N_EXPERTS * Wn, D_MODEL), preferred_element_type=F32)
                + jnp.dot(smat, src[1].reshape(N_EXPERTS * Wn, D_MODEL), preferred_element_type=F32))

    @pl.when(i == 0)
    def _():
        for c in copies(0, 0, buf.at[0], sem.at[0]):
            c.start()

    @pl.when(i + 1 < ntiles)
    def _():
        for c in copies(i + 1, 0, buf.at[1 - slot], sem.at[1 - slot]):
            c.start()

    smat = one_hot(0, False)
    for c in copies(i, 0, buf.at[slot], sem.at[slot]):
        c.wait()
    acc = h_ref[...] + placed(smat, buf[slot])

    def extra_round(k, acc):
        for c in copies(i, k, xbuf, xsem):
            c.start()
        smat = one_hot(k, True)
        for c in copies(i, k, xbuf, xsem):
            c.wait()
        return acc + placed(smat, xbuf[...])

    o_ref[...] = lax.fori_loop(1, nr_ref[i], extra_round, acc)


def _moe_combine(h, ye_hi, ye_lo, pos_t, base, nrounds):
    T = h.shape[0]
    cap = ye_hi.shape[1]
    ntiles = T // CMB_TOKENS
    assert cap >= CMB_WINDOW and cap % CMB_ALIGN == 0
    wshape = (2, N_EXPERTS, CMB_WINDOW, D_MODEL)
    grid_spec = pltpu.PrefetchScalarGridSpec(
        num_scalar_prefetch=2, grid=(ntiles,),
        in_specs=[pl.BlockSpec((CMB_TOKENS, D_MODEL), lambda i, b, n: (i, 0)),
                  pl.BlockSpec((CMB_TOKENS, N_EXPERTS), lambda i, b, n: (i, 0)),
                  pl.BlockSpec(memory_space=pl.ANY), pl.BlockSpec(memory_space=pl.ANY)],
        out_specs=pl.BlockSpec((CMB_TOKENS, D_MODEL), lambda i, b, n: (i, 0)),
        scratch_shapes=[pltpu.VMEM((2,) + wshape, BF16), pltpu.VMEM(wshape, BF16),
                        pltpu.SemaphoreType.DMA((2,)), pltpu.SemaphoreType.DMA(())])
    return pl.pallas_call(
        functools.partial(_combine_body, cap=cap, ntiles=ntiles), grid_spec=grid_spec,
        out_shape=jax.ShapeDtypeStruct((T, D_MODEL), F32),
        compiler_params=_cparams(1), name="moe_combine")(base.reshape(-1), nrounds, h, pos_t, ye_hi, ye_lo)


SEL_ROWS = 4


def _flag(cond):
    return jnp.where(cond, 1.0, 0.0)


def _select_body(aff_ref, pos_ref, idx_ref, gate_ref, thr_sc, *, cap):
    n_exp, n_chunk, _ = aff_ref.shape

    def bit_step(i, prefix):
        cand = prefix | jnp.left_shift(jnp.int32(1), 30 - i)
        hit = _flag(pltpu.bitcast(aff_ref[...], I32) >= cand)
        cnt = jnp.sum(jnp.sum(hit, axis=1, keepdims=True), axis=2, keepdims=True)
        return jnp.where(cnt >= cap, cand, prefix)

    thr = lax.fori_loop(0, 31, bit_step, jnp.zeros((n_exp, 1, 1), I32))
    thr_sc[...] = jnp.broadcast_to(thr, thr_sc.shape)

    li = lax.broadcasted_iota(I32, (LANES, LANES), 0)
    lj = lax.broadcasted_iota(I32, (LANES, LANES), 1)
    upper = _flag(li <= lj).astype(BF16)
    ci = lax.broadcasted_iota(I32, (n_chunk, n_chunk), 0)
    cj = lax.broadcasted_iota(I32, (n_chunk, n_chunk), 1)
    before = _flag(cj < ci).astype(BF16)
    chunk_id = lax.broadcasted_iota(I32, (n_chunk, LANES), 0).astype(F32)
    lane_id = lax.broadcasted_iota(I32, (LANES, LANES), 0).astype(F32)
    slot_lane = lax.broadcasted_iota(I32, (1, LANES), 1)

    def counts(flags):
        local = jnp.dot(flags.astype(BF16), upper, preferred_element_type=F32)
        total = jnp.broadcast_to(local[:, LANES - 1:LANES], local.shape)
        return local, total, jnp.dot(before, total.astype(BF16), preferred_element_type=F32)

    def per_expert(e, carry):
        aff = aff_ref[e]
        key = pltpu.bitcast(aff, I32)
        t = thr_sc[e]
        gt = _flag(key > t)
        eq = _flag(key == t)
        need = cap - jnp.sum(jnp.sum(gt, axis=1, keepdims=True), axis=0, keepdims=True)
        eq_local, _, eq_off = counts(eq)
        sel = gt + eq * _flag(eq_local + eq_off - eq < need)
        local, total, off = counts(sel)
        pos_ref[e] = jnp.where(sel > 0.0, local + off - 1.0, -1.0).astype(I32)
        reached = off + total
        local_t = jnp.transpose(local).astype(BF16)
        aff_tr = jnp.transpose(aff)

        def one_row(r):
            slot = (r * LANES + slot_lane).astype(F32)
            chunk = jnp.sum(_flag(reached <= slot), axis=0, keepdims=True)
            skipped = jnp.sum(jnp.where(chunk_id < chunk, total, 0.0), axis=0, keepdims=True)
            pick = _flag(chunk_id == chunk)
            run = jnp.dot(local_t, pick.astype(BF16), preferred_element_type=F32)
            lane = jnp.sum(_flag(run <= slot - skipped), axis=0, keepdims=True)
            idx_ref[e, pl.ds(r, 1), :] = (chunk * LANES + lane).astype(I32)
            vals = jnp.dot(aff_tr, pick, precision=lax.Precision.HIGHEST, preferred_element_type=F32)
            gate_ref[e, pl.ds(r, 1), :] = jnp.sum(jnp.where(lane_id == lane, vals, 0.0), axis=0, keepdims=True)

        def row_group(g, c2):
            for i in range(SEL_ROWS):
                one_row(g * SEL_ROWS + i)
            return c2

        lax.fori_loop(0, cap // (LANES * SEL_ROWS), row_group, 0)
        return carry

    lax.fori_loop(0, n_exp, per_expert, 0)


def _select(aff_t, cap):
    E, T = aff_t.shape
    assert T % LANES == 0 and cap % (LANES * SEL_ROWS) == 0
    full = lambda shape: pl.BlockSpec(shape, lambda i: (0, 0, 0))
    shapes = [(E, T // LANES, LANES), (E, cap // LANES, LANES), (E, cap // LANES, LANES)]
    pos, idx, gate = pl.pallas_call(
        functools.partial(_select_body, cap=cap), grid=(1,),
        in_specs=[full(shapes[0])], out_specs=[full(s) for s in shapes],
        out_shape=[jax.ShapeDtypeStruct(shapes[0], I32), jax.ShapeDtypeStruct(shapes[1], I32),
                   jax.ShapeDtypeStruct(shapes[2], F32)],
        scratch_shapes=[pltpu.VMEM((E, 1, LANES), I32)],
        compiler_params=_cparams(1), name="moe_select")(aff_t.reshape(shapes[0]))
    return pos.reshape(E, T), idx.reshape(E, cap), gate.reshape(E, cap)


def _route(aff_t, cap):
    E, T = aff_t.shape
    ntiles = T // CMB_TOKENS
    pos, idx_s, gate_s = _select(aff_t, cap)
    cnt = jnp.sum((pos >= 0).reshape(E, ntiles, CMB_TOKENS), axis=-1, dtype=I32)
    base = jnp.cumsum(cnt, axis=1, dtype=I32) - cnt
    need = jnp.where(cnt > 0, (base % CMB_ALIGN + cnt + CMB_WINDOW - 1) // CMB_WINDOW, 0)
    return idx_s, gate_s, pos.T, base, jnp.max(need, axis=0).astype(I32)


def _moe_layer(h, u, aff_t, wg, wu, wd, layer):
    T = h.shape[0]
    cap = CAPACITY_FACTOR * T // N_EXPERTS
    idx_s, gate_s, pos_t, base, nrounds = _route(aff_t, cap)
    ye_hi, ye_lo = _moe_experts(u, idx_s, gate_s, wg, wu, wd, layer)
    return _moe_combine(h, ye_hi, ye_lo, pos_t, base, nrounds)


def _rope_tables(seq_len, n_freq, reps):
    t = jnp.arange(seq_len)
    inv_freq = ROPE_THETA ** (-jnp.arange(n_freq, dtype=F32) / n_freq)
    out = []
    for pos in ((t // GRID_W).astype(F32), (t % GRID_W).astype(F32)):
        ang = pos[:, None] * inv_freq[None, :]
        c = jnp.concatenate([jnp.cos(ang), jnp.cos(ang)], axis=1)
        s = jnp.concatenate([-jnp.sin(ang), jnp.sin(ang)], axis=1)
        out.append((c, s))
    cos = jnp.concatenate([out[0][0], out[1][0]], axis=1)
    sin = jnp.concatenate([out[0][1], out[1][1]], axis=1)
    return jnp.tile(cos, (1, reps)), jnp.tile(sin, (1, reps))


def _dup_heads(w, n_heads, dh):
    k = w.shape[0]
    return jnp.repeat(w.reshape(k, n_heads, 1, dh), 2, axis=2).reshape(k, n_heads * 2 * dh)


def _trunk(x, p):
    bsz, seq_len, _ = x.shape
    T = bsz * seq_len
    h = x.reshape(T, D_MODEL)
    for layer in range(4):
        nm = p["norm_mix"][layer]
        if layer == 0:
            qk = _proj(h, nm, p["na_wqk"], seq_len, BF16, headnorm=True, colvec=p["na_qk_gain"], name="na_proj_qk")
            v = _proj(h, nm, p["na_wv"], seq_len, BF16, name="na_proj_v")
            mix_args = (_na_attention(qk, v, p["na_bt"], bsz, seq_len),)
            kind, w_out = "plain", p["na_wo"]
        elif layer == 1:
            y = _proj(h, nm, p["hg_win"], seq_len, F32, name="hg_proj")
            o_f = _hgrn_direction(y, p["hg_lb"], bsz, seq_len, False)
            o_b = _hgrn_direction(y, p["hg_lb"], bsz, seq_len, True)
            mix_args = (o_f, o_b, y, 4, p["hg_o_gain"])
            kind, w_out = "hg", p["hg_wo"]
        elif layer == 2:
            cos, sin = _rope_tables(seq_len, 16, 2)
            qk = _proj(h, nm, p["gq_wqk"], seq_len, BF16, headnorm=True, colvec=p["gq_qk_gain"],
                       rope=(16, cos, sin), name="gqa_proj_qk")
            v = _proj(h, nm, p["gq_wv"], seq_len, BF16, name="gqa_proj_v")
            mix_args = (_flash_gqa(qk, v, bsz, seq_len),)
            kind, w_out = "plain", p["gq_wo"]
        else:
            cos, sin = _rope_tables(seq_len, 64, 1)
            qk = _proj(h, nm, p["rt_wqk"], seq_len, BF16, colvec=p["rt_qk_scale"],
                       rope=(64, cos, sin), name="ret_proj_qk")
            v = _proj(h, nm, p["rt_wv"], seq_len, BF16, name="ret_proj_v")
            g = _proj(h, nm, p["rt_wg"], seq_len, F32, name="ret_proj_g")
            o_f = _ret_direction(qk, v, bsz, seq_len, False)
            o_b = _ret_direction(qk, v, bsz, seq_len, True)
            mix_args = (o_f, o_b, g, 0, p["rt_o_gain"])
            kind, w_out = "rt", p["rt_wo"]
        h, u, aff_t = _outproj(h, mix_args, w_out, p["norm_ffn"][layer], p["router_t"][layer], kind,
                               name=f"outproj_{kind}")
        h = _moe_layer(h, u, aff_t, p["moe_wg"], p["moe_wu"], p["moe_wd"], layer)
    return h.reshape(bsz, seq_len, D_MODEL)


def kernel(x_prompt, x_sample, norm_mix, norm_ffn, na_w_in, na_q_gain, na_k_gain, na_rel_bias, na_w_out, hg_w_in, hg_lb, hg_o_gain, hg_w_out, gq_w_in, gq_q_gain, gq_k_gain, gq_w_out, rt_w_in, rt_o_gain, rt_w_out, moe_router, moe_w_gate, moe_w_up, moe_w_down):
    bf = lambda a: a.astype(BF16)
    na_w, gq_w, rt_w = na_w_in[0], gq_w_in[0], rt_w_in[0]
    lb_cum = jnp.cumsum(jax.nn.softmax(hg_lb.astype(F32), axis=0), axis=0)
    qd = GQA_HEADS * GQA_DH
    kd = GQA_KV * GQA_DH
    rq = RET_HEADS * RET_DK
    rv = RET_HEADS * RET_DV
    p = {
        "norm_mix": norm_mix, "norm_ffn": norm_ffn,
        "na_wqk": bf(na_w[:, :2 * D_MODEL]), "na_wv": bf(na_w[:, 2 * D_MODEL:]),
        "na_qk_gain": jnp.concatenate([jnp.tile(na_q_gain[0], NA_HEADS) * (64 ** -0.5 * LOG2E),
                                       jnp.tile(na_k_gain[0], NA_HEADS)]),
        "na_bt": _na_bias_table(na_rel_bias[0]), "na_wo": bf(na_w_out[0]),
        "hg_win": bf(hg_w_in[0]), "hg_lb": lb_cum[1] - lb_cum[0], "hg_o_gain": hg_o_gain[0],
        "hg_wo": bf(hg_w_out[0]),
        "gq_wqk": bf(jnp.concatenate([gq_w[:, :qd], _dup_heads(gq_w[:, qd:qd + kd], GQA_KV, GQA_DH)], axis=1)),
        "gq_wv": bf(_dup_heads(gq_w[:, qd + kd:], GQA_KV, GQA_DH)),
        "gq_qk_gain": jnp.concatenate([jnp.tile(gq_q_gain[0], GQA_HEADS) * (GQA_DH ** -0.5 * LOG2E),
                                       jnp.tile(gq_k_gain[0], 2 * GQA_KV)]),
        "gq_wo": bf(gq_w_out[0]),
        "rt_wqk": bf(rt_w[:, :2 * rq]), "rt_wv": bf(rt_w[:, 2 * rq:2 * rq + rv]), "rt_wg": bf(rt_w[:, 2 * rq + rv:]),
        "rt_qk_scale": jnp.concatenate([jnp.full((rq,), RET_DK ** -0.5, F32), jnp.ones((rq,), F32)]),
        "rt_o_gain": rt_o_gain[0], "rt_wo": bf(rt_w_out[0]),
        "router_t": jnp.swapaxes(moe_router, 1, 2).astype(F32),
        "moe_wg": bf(moe_w_gate), "moe_wu": bf(moe_w_up), "moe_wd": bf(moe_w_down),
    }
    return (_trunk(x_prompt, p), _trunk(x_sample, p))
```
